```python
import jax
import jax.numpy as jnp
from jax import lax
import numpy as np

D_MODEL = 2048
BATCH = 2
SEQ = 4096
DEPTH = 2
DEC_BATCH = 32
DEC_SEQ = 1
PAST_LEN = 8192
PAGE_SIZE = 128

DH = 128
WA = 3 * D_MODEL // 8
WB = 3 * D_MODEL // 8
WC = D_MODEL - WA - WB
MIX = WA + WB + WC
HA = WA // DH
HB = WB // DH
DC = 64
HC = WC // DC
HI = 16
DI = 64
TOPK = 256
ROPE_THETA = 500000.0
ROT_DIM = DH // 4
ROT_DIM_IDX = DI // 4
LORA_W = 32
LORA_A = 32
Q_BLOCK = 128
RMS_EPS = 1e-6
GN_EPS = 64e-5
FORGET_BIAS_INIT = 3.0
C_SHIFT = 3 * WC + LORA_W + LORA_A
SEGMENTS = (('fa_q', WA), ('fa_k', WA), ('fa_v', WA), ('fa_f', HA), ('fa_g', WA),
            ('sb_q', WB), ('sb_k', WB), ('sb_v', WB), ('sb_g', WB),
            ('ix_q', HI * DI), ('ix_k', DI), ('ix_w', HI),
            ('rc_shift', C_SHIFT), ('rc_g', WC))
N_IN = 4 * WA + HA + 4 * WB + HI * DI + DI + HI + C_SHIFT + WC

kernel_name = 'hybrid_fox_dsa_rwkv7_step'

F32 = jnp.float32


def _split(z):
    out = {}
    off = 0
    for name, w in SEGMENTS:
        out[name] = z[..., off:off + w]
        off += w
    return out


def _heads(t, h, d):
    return t.reshape(t.shape[:-1] + (h, d))


def _rmsnorm(x, g):
    xf = x.astype(F32)
    y = xf * lax.rsqrt(jnp.mean(xf * xf, axis=-1, keepdims=True) + RMS_EPS)
    return (y * g.astype(F32)).astype(x.dtype)


def _rope(x, pos, rot):
    half = rot // 2
    inv = ROPE_THETA ** (-jnp.arange(half, dtype=F32) * (2.0 / rot))
    ang = pos.astype(F32)[:, None] * inv[None, :]
    cos = jnp.cos(ang)[:, None, :]
    sin = jnp.sin(ang)[:, None, :]
    xf = x.astype(F32)
    x1, x2 = xf[..., :half], xf[..., half:rot]
    rot_part = jnp.concatenate([x1 * cos - x2 * sin, x2 * cos + x1 * sin], axis=-1)
    return jnp.concatenate([rot_part.astype(x.dtype), x[..., rot:]], axis=-1)


def _fox_inputs(seg, b_f):
    q = _heads(seg['fa_q'], HA, DH)
    k = _heads(seg['fa_k'], HA, DH)
    v = _heads(seg['fa_v'], HA, DH)
    logf = jax.nn.log_sigmoid(seg['fa_f'].astype(F32) + b_f.astype(F32))
    return q, k, v, logf


def _fox_prompt(q, k, v, c):
    B, S = q.shape[:2]
    nb = S // Q_BLOCK
    scale = DH ** -0.5
    qb = q.reshape(B, nb, Q_BLOCK, HA, DH).swapaxes(0, 1)
    cb = c.reshape(B, nb, Q_BLOCK, HA).swapaxes(0, 1)
    kpos = jnp.arange(S)
    ck = c.transpose(0, 2, 1)[:, :, None, :]

    def block(args):
        qi, ci, i = args
        qpos = i * Q_BLOCK + jnp.arange(Q_BLOCK)
        s = jnp.einsum('bqhd,bkhd->bhqk', qi, k).astype(F32) * scale
        s = s + ci.transpose(0, 2, 1)[..., None] - ck
        s = jnp.where(qpos[:, None] >= kpos[None, :], s, -jnp.inf)
        p = jax.nn.softmax(s, axis=-1)
        return jnp.einsum('bhqk,bkhd->bqhd', p.astype(v.dtype), v)

    o = lax.map(block, (qb, cb, jnp.arange(nb)))
    return o.swapaxes(0, 1).reshape(B, S, HA * DH)


def _fox_sample(q, k, v, logf, cache_k, cache_v, cache_logf, page_table, l):
    DB, T = q.shape[:2]
    n_pages = page_table.shape[1]
    past = n_pages * PAGE_SIZE
    scale = DH ** -0.5
    logf_past = cache_logf[l, page_table].reshape(DB, past, HA).astype(F32)
    c_past = jnp.cumsum(logf_past, axis=1)
    c_new = c_past[:, -1:] + jnp.cumsum(logf, axis=1)
    cq = c_new.transpose(0, 2, 1)[..., None]
    s = jnp.einsum('bqhd,bkhd->bhqk', q, k).astype(F32) * scale
    s = s + cq - c_new.transpose(0, 2, 1)[:, :, None, :]
    causal = jnp.arange(T)[:, None] >= jnp.arange(T)[None, :]
    s = jnp.where(causal, s, -jnp.inf)
    m = jnp.max(s, axis=-1)
    p = jnp.exp(s - m[..., None])
    lsum = jnp.sum(p, axis=-1)
    acc = jnp.einsum('bhqk,bkhd->bhqd', p, v.astype(F32))

    def step(carry, xs):
        m, lsum, acc = carry
        pt, cp = xs
        kp = cache_k[l, pt]
        vp = cache_v[l, pt]
        s = jnp.einsum('bqhd,bkhd->bhqk', q, kp).astype(F32) * scale
        s = s + cq - cp.astype(F32).transpose(0, 2, 1)[:, :, None, :]
        m_new = jnp.maximum(m, jnp.max(s, axis=-1))
        corr = jnp.exp(m - m_new)
        p = jnp.exp(s - m_new[..., None])
        lsum = lsum * corr + jnp.sum(p, axis=-1)
        acc = acc * corr[..., None] + jnp.einsum('bhqk,bkhd->bhqd', p, vp.astype(F32))
        return (m_new, lsum, acc), None

    xs = (page_table.T, c_past.reshape(DB, n_pages, PAGE_SIZE, HA).swapaxes(0, 1))
    (m, lsum, acc), _ = lax.scan(step, (m, lsum, acc), xs)
    o = (acc / lsum[..., None]).astype(v.dtype)
    return o.transpose(0, 2, 1, 3).reshape(DB, T, HA * DH)


def _dsa_inputs(seg, pos):
    q = _rope(_heads(seg['sb_q'], HB, DH), pos, ROT_DIM)
    k = _rope(_heads(seg['sb_k'], HB, DH), pos, ROT_DIM)
    v = _heads(seg['sb_v'], HB, DH)
    qi = _rope(_heads(seg['ix_q'], HI, DI), pos, ROT_DIM_IDX)
    ki = _rope(seg['ix_k'][..., None, :], pos, ROT_DIM_IDX)[..., 0, :]
    wi = seg['ix_w'].astype(F32) * HI ** -0.5
    return q, k, v, qi, ki, wi


def _index_scores(qi, wi, ki):
    dots = jnp.einsum('bqhd,bkd->bqhk', qi, ki).astype(F32) * DI ** -0.5
    return jnp.einsum('bqhk,bqh->bqk', jax.nn.relu(dots), wi)


def _sparse_attend(q, kg, vg, valid):
    s = jnp.einsum('bqhd,bqjhd->bhqj', q, kg).astype(F32) * DH ** -0.5
    s = jnp.where(valid[:, None], s, -jnp.inf)
    p = jax.nn.softmax(s, axis=-1)
    o = jnp.einsum('bhqj,bqjhd->bqhd', p.astype(vg.dtype), vg)
    return o.reshape(o.shape[:2] + (HB * DH,))


def _dsa_prompt(q, k, v, qi, ki, wi):
    B, S = q.shape[:2]
    nb = S // Q_BLOCK
    n_sel = min(TOPK, S // 4)
    kpos = jnp.arange(S)

    def blk(t):
        return t.reshape((B, nb, Q_BLOCK) + t.shape[2:]).swapaxes(0, 1)

    def block(args):
        qb, qib, wib, i = args
        qpos = i * Q_BLOCK + jnp.arange(Q_BLOCK)
        sc = _index_scores(qib, wib, ki)
        sc = jnp.where(qpos[:, None] >= kpos[None, :], sc, -jnp.inf)
        _, idx = lax.top_k(sc, n_sel)
        valid = idx <= qpos[None, :, None]
        kg = jax.vmap(lambda kb, ib: kb[ib])(k, idx)
        vg = jax.vmap(lambda vb, ib: vb[ib])(v, idx)
        return _sparse_attend(qb, kg, vg, valid)

    o = lax.map(block, (blk(q), blk(qi), blk(wi), jnp.arange(nb)))
    return o.swapaxes(0, 1).reshape(B, S, HB * DH)


def _dsa_sample(q, k, v, qi, ki, wi, cache_k, cache_v, cache_ki, page_table, l):
    DB, T = q.shape[:2]
    n_pages = page_table.shape[1]
    past = n_pages * PAGE_SIZE
    n_keys = past + T
    n_sel = min(TOPK, n_keys // 4)
    ki_past = cache_ki[l, page_table].reshape(DB, past, DI).astype(ki.dtype)
    ki_all = jnp.concatenate([ki_past, ki], axis=1)
    qpos = past + jnp.arange(T)
    kpos = jnp.arange(n_keys)
    sc = _index_scores(qi, wi, ki_all)
    sc = jnp.where(qpos[:, None] >= kpos[None, :], sc, -jnp.inf)
    _, idx = lax.top_k(sc, n_sel)
    valid = idx <= qpos[None, :, None]
    in_past = idx < past
    idx_p = jnp.minimum(idx, past - 1)
    phys = jax.vmap(lambda pt, i: pt[i])(page_table, idx_p // PAGE_SIZE)
    off = idx_p % PAGE_SIZE
    idx_n = jnp.clip(idx - past, 0, T - 1)

    def gather(cache, new):
        from_past = cache[l, phys, off].astype(new.dtype)
        from_new = jax.vmap(lambda nb, ib: nb[ib])(new, idx_n)
        return jnp.where(in_past[..., None, None], from_past, from_new)

    return _sparse_attend(q, gather(cache_k, k), gather(cache_v, v), valid)


def _rwkv_mix(shift_cols, shift_prev, wkv0, mu, w0, w_b, a0, a_b, k_k, k_a, r_k, ln_w, ln_b):
    B, T = shift_cols.shape[:2]
    cur = shift_cols.astype(F32)
    prev = jnp.concatenate([shift_prev[:, None].astype(F32), cur[:, :-1]], axis=1)
    xm = cur + (prev - cur) * mu.astype(F32)
    r = xm[..., :WC]
    k = xm[..., WC:2 * WC]
    v = xm[..., 2 * WC:3 * WC]
    wl = xm[..., 3 * WC:3 * WC + LORA_W]
    al = xm[..., 3 * WC + LORA_W:]
    w_raw = w0.astype(F32) + jnp.tanh(wl) @ w_b.astype(F32)
    decay = jnp.exp(-jnp.exp(-jax.nn.softplus(-w_raw) - 0.5))
    a = jax.nn.sigmoid(a0.astype(F32) + al @ a_b.astype(F32))
    kk = _heads(k * k_k.astype(F32), HC, DC)
    kk = kk / jnp.maximum(jnp.sqrt(jnp.sum(kk * kk, axis=-1, keepdims=True)), 1e-12)
    k = k * (1.0 + (a - 1.0) * k_a.astype(F32))
    r, k, v, decay, a = (_heads(t, HC, DC) for t in (r, k, v, decay, a))

    def step(S, xs):
        r_t, k_t, v_t, w_t, kk_t, a_t = xs
        sa = jnp.einsum('bhij,bhj->bhi', S, -kk_t)
        S = (S * w_t[:, :, None, :] + sa[..., None] * (kk_t * a_t)[:, :, None, :]
             + v_t[..., None] * k_t[:, :, None, :])
        return S, jnp.einsum('bhij,bhj->bhi', S, r_t)

    xs = tuple(t.swapaxes(0, 1) for t in (r, k, v, decay, kk, a))
    S, out = lax.scan(step, wkv0.astype(F32), xs)
    out = out.swapaxes(0, 1)
    mean = jnp.mean(out, axis=-1, keepdims=True)
    var = jnp.mean(jnp.square(out - mean), axis=-1, keepdims=True)
    out = ((out - mean) * lax.rsqrt(var + GN_EPS)).reshape(B, T, WC)
    out = out * ln_w.astype(F32) + ln_b.astype(F32)
    bonus = jnp.sum(r * k * r_k.astype(F32), axis=-1, keepdims=True) * v
    out = out + bonus.reshape(B, T, WC)
    return out, S, shift_cols[:, -1]


def _merge(seg, o_a, o_b, o_c, w_o, dtype):
    y = jnp.concatenate([o_a.astype(dtype) * jax.nn.silu(seg['fa_g']),
                         o_b.astype(dtype) * jax.nn.silu(seg['sb_g']),
                         o_c.astype(dtype) * jax.nn.silu(seg['rc_g'])], axis=-1)
    return y @ w_o


def _layer_prompt(x, g, w_i, b_f, rw, w_o):
    B, S = x.shape[:2]
    seg = _split(_rmsnorm(x, g) @ w_i)
    qa, ka, va, logf = _fox_inputs(seg, b_f)
    o_a = _fox_prompt(qa, ka, va, jnp.cumsum(logf, axis=1))
    qb, kb, vb, qi, ki, wi = _dsa_inputs(seg, jnp.arange(S))
    o_b = _dsa_prompt(qb, kb, vb, qi, ki, wi)
    o_c, wkv, shift = _rwkv_mix(seg['rc_shift'], jnp.zeros((B, C_SHIFT), F32),
                                jnp.zeros((B, HC, DC, DC), F32), *rw)
    y = x + _merge(seg, o_a, o_b, o_c, w_o, x.dtype)
    return y, (ka, va, logf, kb, vb, ki, wkv, shift)


def _layer_sample(x, l, caches, page_table, g, w_i, b_f, rw, w_o):
    c_fk, c_fv, c_fl, c_dk, c_dv, c_di, s_wkv, s_shift = caches
    T = x.shape[1]
    past = page_table.shape[1] * PAGE_SIZE
    seg = _split(_rmsnorm(x, g) @ w_i)
    qa, ka, va, logf = _fox_inputs(seg, b_f)
    o_a = _fox_sample(qa, ka, va, logf, c_fk, c_fv, c_fl, page_table, l)
    qb, kb, vb, qi, ki, wi = _dsa_inputs(seg, past + jnp.arange(T))
    o_b = _dsa_sample(qb, kb, vb, qi, ki, wi, c_dk, c_dv, c_di, page_table, l)
    o_c, wkv, shift = _rwkv_mix(seg['rc_shift'], s_shift[l], s_wkv[l], *rw)
    y = x + _merge(seg, o_a, o_b, o_c, w_o, x.dtype)
    return y, (ka, va, logf, kb, vb, ki, wkv, shift)


def setup_inputs(seed: int = 0) -> dict:
    key = jax.random.key(seed)
    ks = jax.random.split(key, 32)
    n_pages = PAST_LEN // PAGE_SIZE
    n_used = DEC_BATCH * n_pages
    n_pool = n_used + max(1, n_used // 4)

    def nrm(k, shape, s=1.0):
        return s * jax.random.normal(k, shape, F32)

    page_table = jax.random.permutation(ks[0], n_pool)[:n_used].reshape(DEC_BATCH, n_pages).astype(jnp.int32)
    return {
        'x_prompt': nrm(ks[1], (BATCH, SEQ, D_MODEL)),
        'x_sample': nrm(ks[2], (DEC_BATCH, DEC_SEQ, D_MODEL)),
        'cache_fox_k': nrm(ks[3], (DEPTH, n_pool, PAGE_SIZE, HA, DH)),
        'cache_fox_v': nrm(ks[4], (DEPTH, n_pool, PAGE_SIZE, HA, DH)),
        'cache_fox_logf': jax.nn.log_sigmoid(FORGET_BIAS_INIT + nrm(ks[5], (DEPTH, n_pool, PAGE_SIZE, HA))),
        'cache_dsa_k': nrm(ks[6], (DEPTH, n_pool, PAGE_SIZE, HB, DH)),
        'cache_dsa_v': nrm(ks[7], (DEPTH, n_pool, PAGE_SIZE, HB, DH)),
        'cache_dsa_kidx': nrm(ks[8], (DEPTH, n_pool, PAGE_SIZE, DI)),
        'state_rwkv_wkv': nrm(ks[9], (DEPTH, DEC_BATCH, HC, DC, DC), 0.5),
        'state_rwkv_shift': nrm(ks[10], (DEPTH, DEC_BATCH, C_SHIFT)),
        'page_table': page_table,
        'norm_gain': 1.0 + nrm(ks[11], (DEPTH, D_MODEL), 0.1),
        'w_in': nrm(ks[12], (DEPTH, D_MODEL, N_IN), D_MODEL ** -0.5),
        'fox_forget_bias': FORGET_BIAS_INIT + nrm(ks[13], (DEPTH, HA), 0.1),
        'rwkv_mu': jax.random.uniform(ks[14], (DEPTH, C_SHIFT), F32),
        'rwkv_w0': jax.random.uniform(ks[15], (DEPTH, WC), F32, -4.0, 1.0),
        'rwkv_w_lora_b': nrm(ks[16], (DEPTH, LORA_W, WC), 0.5 * LORA_W ** -0.5),
        'rwkv_a0': nrm(ks[17], (DEPTH, WC), 0.1),
        'rwkv_a_lora_b': nrm(ks[18], (DEPTH, LORA_A, WC), 0.5 * LORA_A ** -0.5),
        'rwkv_k_k': 0.85 + nrm(ks[19], (DEPTH, WC), 0.1),
        'rwkv_k_a': 1.0 + nrm(ks[20], (DEPTH, WC), 0.1),
        'rwkv_r_k': nrm(ks[21], (DEPTH, HC, DC), 0.1),
        'rwkv_ln_w': 1.0 + nrm(ks[22], (DEPTH, WC), 0.1),
        'rwkv_ln_b': nrm(ks[23], (DEPTH, WC), 0.01),
        'w_out': nrm(ks[24], (DEPTH, MIX, D_MODEL), MIX ** -0.5),
        'final_gain': 1.0 + nrm(ks[25], (D_MODEL,), 0.1),
    }


def reference(x_prompt, x_sample, cache_fox_k, cache_fox_v, cache_fox_logf, cache_dsa_k, cache_dsa_v,
              cache_dsa_kidx, state_rwkv_wkv, state_rwkv_shift, page_table, norm_gain, w_in,
              fox_forget_bias, rwkv_mu, rwkv_w0, rwkv_w_lora_b, rwkv_a0, rwkv_a_lora_b, rwkv_k_k,
              rwkv_k_a, rwkv_r_k, rwkv_ln_w, rwkv_ln_b, w_out, final_gain):
    caches = (cache_fox_k, cache_fox_v, cache_fox_logf, cache_dsa_k, cache_dsa_v, cache_dsa_kidx,
              state_rwkv_wkv, state_rwkv_shift)
    xp, xs = x_prompt, x_sample
    new_p, new_s = [], []
    for l in range(DEPTH):
        rw = (rwkv_mu[l], rwkv_w0[l], rwkv_w_lora_b[l], rwkv_a0[l], rwkv_a_lora_b[l], rwkv_k_k[l],
              rwkv_k_a[l], rwkv_r_k[l], rwkv_ln_w[l], rwkv_ln_b[l])
        xp, st_p = _layer_prompt(xp, norm_gain[l], w_in[l], fox_forget_bias[l], rw, w_out[l])
        xs, st_s = _layer_sample(xs, l, caches, page_table, norm_gain[l], w_in[l], fox_forget_bias[l], rw, w_out[l])
        new_p.append(st_p)
        new_s.append(st_s)
    p_fk, p_fv, p_fl, p_dk, p_dv, p_di, p_wkv, p_sh = [jnp.stack(a) for a in zip(*new_p)]
    s_fk, s_fv, s_fl, s_dk, s_dv, s_di, s_wkv, s_sh = [jnp.stack(a) for a in zip(*new_s)]
    y_prompt = _rmsnorm(xp, final_gain)
    y_sample = _rmsnorm(xs, final_gain)
    return (y_prompt, y_sample, p_fk, p_fv, p_fl, p_dk, p_dv, p_di, p_wkv, p_sh,
            s_fk, s_fv, s_fl, s_dk, s_dv, s_di, s_wkv, s_sh)
```

```python
import functools

import jax
import jax.numpy as jnp
from jax import lax
from jax.experimental import pallas as pl
from jax.experimental.pallas import tpu as pltpu

F32 = jnp.float32
BF16 = jnp.bfloat16
HIGHEST = lax.Precision.HIGHEST

DH = 128
HA = 6
HB = 6
WA = HA * DH
WB = HB * DH
DC = 64
HC = 8
WC = HC * DC
HI = 16
DI = 64
TOPK = 256
ROPE_THETA = 500000.0
ROT_DIM = DH // 4
ROT_DIM_IDX = DI // 4
LORA = 32
PAGE = 128
RMS_EPS = 1e-6
GN_EPS = 64e-5
C_SHIFT = 3 * WC + 2 * LORA
LANE = 128
NEG = -1e30
INT_MIN = -(2 ** 31)
VMEM_LIMIT = 56 * 1024 * 1024

Z_FA_Q, Z_FA_K, Z_FA_V, Z_FA_G = 0, 768, 1536, 2304
Z_SB_V, Z_SB_G, Z_SB_Q, Z_SB_K = 3072, 3840, 4608, 5376
Z_IX_Q = 6144
Z_RC_G, Z_RC_R, Z_RC_K, Z_RC_V = 7168, 7680, 8192, 8704
Z_RC_WA = 9216
Z_MISC = 9344
NZ = 9728
M_IXK, M_IXW, M_F = 0, 64, 80
TN = 512
TILE_KINDS = ((0, 9, 0), (9, 12, 1), (12, 14, 2), (14, 18, 0), (18, 19, 3))
RC_CHUNK = 64
INPROJ_TM = 512
OUT_TM = 256
CUM_TM = 256
FLASH_TQ = 512
SEL_TQ = 256
PPS_ATTN = 8
PPS_IDX = 16


def _cparams(sem):
    return pltpu.CompilerParams(dimension_semantics=sem, vmem_limit_bytes=VMEM_LIMIT)


def _relayout_w_in(w):
    d = w.shape[0]
    o = {}
    off = 0
    for name, width in (('fa_q', WA), ('fa_k', WA), ('fa_v', WA), ('fa_f', HA), ('fa_g', WA),
                        ('sb_q', WB), ('sb_k', WB), ('sb_v', WB), ('sb_g', WB),
                        ('ix_q', HI * DI), ('ix_k', DI), ('ix_w', HI),
                        ('rc_shift', C_SHIFT), ('rc_g', WC)):
        o[name] = w[:, off:off + width]
        off += width
    zeros = lambda n: jnp.zeros((d, n), w.dtype)
    rs = o['rc_shift']
    parts = [o['fa_q'], o['fa_k'], o['fa_v'], o['fa_g'], o['sb_v'], o['sb_g'], o['sb_q'], o['sb_k'],
             o['ix_q'], o['rc_g'], rs[:, :3 * WC], rs[:, 3 * WC:], zeros(64),
             o['ix_k'], o['ix_w'], o['fa_f'], zeros(LANE - DI - HI - HA), zeros(NZ - Z_MISC - LANE)]
    wz = jnp.concatenate(parts, axis=1)
    assert wz.shape[1] == NZ
    return wz.astype(BF16)


def _rope_tables(pos):
    lane = jnp.arange(LANE)
    posf = pos.astype(F32)[:, None]

    def tab(head_w, rot, lane_limit):
        half = rot // 2
        lh = lane % head_w
        inv = ROPE_THETA ** (-(lh % half).astype(F32) * (2.0 / rot))
        ang = posf * inv[None, :]
        c, s = jnp.cos(ang), jnp.sin(ang)
        act = (lh < rot) & (lane < lane_limit)
        cos_t = jnp.where(act[None, :], c, 1.0)
        sin_lo = jnp.where((act & (lh >= half))[None, :], s, 0.0)
        sin_hi = jnp.where((act & (lh < half))[None, :], -s, 0.0)
        return [cos_t, sin_lo, sin_hi]

    return jnp.stack(tab(DH, ROT_DIM, LANE) + tab(DI, ROT_DIM_IDX, LANE) + tab(DI, ROT_DIM_IDX, DI), axis=0)


def _log_sigmoid(x):
    return jnp.minimum(x, 0.0) - jnp.log(1.0 + jnp.exp(-jnp.abs(x)))


def _rope_chunk(x, tab_ref, kind, half):
    c = tab_ref[3 * kind + 0]
    s_lo = tab_ref[3 * kind + 1]
    s_hi = tab_ref[3 * kind + 2]
    return x * c + pltpu.roll(x, half, 1) * s_lo + pltpu.roll(x, LANE - half, 1) * s_hi


def _inproj_kernel(x_ref, g_ref, w_ref, tab_ref, bf_ref, z_ref, xn_ref):
    j = pl.program_id(1)

    @pl.when(j == 0)
    def _():
        x = x_ref[...]
        ms = jnp.mean(x * x, axis=-1, keepdims=True)
        xn_ref[...] = (x * lax.rsqrt(ms + RMS_EPS) * g_ref[...]).astype(BF16)

    acc = jnp.dot(xn_ref[...], w_ref[...], preferred_element_type=F32)
    n_chunk = TN // LANE
    for lo, hi, kind in TILE_KINDS:
        @pl.when((j >= lo) & (j < hi))
        def _(kind=kind):
            if kind == 0:
                z_ref[...] = acc
            elif kind in (1, 2):
                half = ROT_DIM // 2 if kind == 1 else ROT_DIM_IDX // 2
                for c in range(n_chunk):
                    sl = slice(c * LANE, (c + 1) * LANE)
                    z_ref[:, sl] = _rope_chunk(acc[:, sl], tab_ref, kind - 1, half)
            else:
                z_ref[...] = acc
                c = (Z_MISC - Z_RC_WA) // LANE
                sl = slice(c * LANE, (c + 1) * LANE)
                xm = acc[:, sl]
                roped = _rope_chunk(xm, tab_ref, 2, ROT_DIM_IDX // 2)
                lane = lax.broadcasted_iota(jnp.int32, xm.shape, 1)
                is_f = (lane >= M_F) & (lane < M_F + HA)
                z_ref[:, sl] = jnp.where(is_f, _log_sigmoid(xm + bf_ref[...]), roped)


def _inproj(x2d, gain, wz, tabs, bf_row, tm):
    rows, d = x2d.shape
    n_pos_blk = tabs.shape[1] // tm
    grid = (rows // tm, NZ // TN)
    return pl.pallas_call(
        _inproj_kernel,
        grid=grid,
        in_specs=[
            pl.BlockSpec((tm, d), lambda i, j: (i, 0)),
            pl.BlockSpec((1, d), lambda i, j: (0, 0)),
            pl.BlockSpec((d, TN), lambda i, j: (0, j)),
            pl.BlockSpec((9, tm, LANE), lambda i, j: (0, i % n_pos_blk, 0)),
            pl.BlockSpec((1, LANE), lambda i, j: (0, 0)),
        ],
        out_specs=pl.BlockSpec((tm, TN), lambda i, j: (i, j)),
        out_shape=jax.ShapeDtypeStruct((rows, NZ), F32),
        scratch_shapes=[pltpu.VMEM((tm, d), BF16)],
        compiler_params=_cparams(("parallel", "arbitrary")),
    )(x2d, gain.reshape(1, d), wz, tabs, bf_row)


def _outproj_kernel(oa_ref, ob_ref, oc_ref, x_ref, w_ref, fg_ref, y_ref, *, final):
    acc = jnp.dot(oa_ref[...].astype(BF16), w_ref[0:WA, :], preferred_element_type=F32)
    acc += jnp.dot(ob_ref[...].astype(BF16), w_ref[WA:WA + WB, :], preferred_element_type=F32)
    acc += jnp.dot(oc_ref[...].astype(BF16), w_ref[WA + WB:, :], preferred_element_type=F32)
    y = x_ref[...] + acc
    if final:
        ms = jnp.mean(y * y, axis=-1, keepdims=True)
        y = y * lax.rsqrt(ms + RMS_EPS) * fg_ref[...]
    y_ref[...] = y


def _outproj(oa, ob, oc, x2d, w_bf, fgain, tm, final):
    rows, d = x2d.shape
    row_spec = lambda w: pl.BlockSpec((tm, w), lambda i: (i, 0))
    return pl.pallas_call(
        functools.partial(_outproj_kernel, final=final),
        grid=(rows // tm,),
        in_specs=[row_spec(WA), row_spec(WB), row_spec(WC), row_spec(d),
                  pl.BlockSpec((WA + WB + WC, d), lambda i: (0, 0)),
                  pl.BlockSpec((1, d), lambda i: (0, 0))],
        out_specs=row_spec(d),
        out_shape=jax.ShapeDtypeStruct((rows, d), F32),
        compiler_params=_cparams(("parallel",)),
    )(oa, ob, oc, x2d, w_bf, fgain.reshape(1, d))


def _cumsum_kernel(z_ref, c_ref, ct_ref, carry_ref):
    @pl.when(pl.program_id(1) == 0)
    def _():
        carry_ref[...] = jnp.zeros_like(carry_ref)

    lf = z_ref[...]
    tm = lf.shape[0]
    r = lax.broadcasted_iota(jnp.int32, (tm, tm), 0)
    c = lax.broadcasted_iota(jnp.int32, (tm, tm), 1)
    tri = jnp.where(r >= c, 1.0, 0.0).astype(F32)
    cum = jnp.dot(tri, lf, precision=HIGHEST, preferred_element_type=F32) + carry_ref[...]
    carry_ref[...] = cum[tm - 1:tm, :]
    c_ref[...] = cum
    ct_ref[0] = cum.T[M_F:M_F + 8, :]


def _fox_cumsum(z, batch, seq, tm=256):
    nblk = seq // tm
    return pl.pallas_call(
        _cumsum_kernel,
        grid=(batch, nblk),
        in_specs=[pl.BlockSpec((tm, LANE), lambda b, i: (b * nblk + i, Z_MISC // LANE))],
        out_specs=[pl.BlockSpec((tm, LANE), lambda b, i: (b * nblk + i, 0)),
                   pl.BlockSpec((1, 8, tm), lambda b, i: (b, 0, i))],
        out_shape=[jax.ShapeDtypeStruct((batch * seq, LANE), F32),
                   jax.ShapeDtypeStruct((batch, 8, seq), F32)],
        scratch_shapes=[pltpu.VMEM((1, LANE), F32)],
        compiler_params=_cparams(("parallel", "arbitrary")),
    )(z)


def _silu(g):
    return g / (1.0 + jnp.exp(-g))


def _flash_kernel(*refs, mode, nh, tq):
    if mode == 'fox':
        q_ref, k_ref, v_ref, g_ref, cq_ref, ck_ref, o_ref, qs_ref, m_ref, l_ref, acc_ref = refs
    else:
        q_ref, k_ref, v_ref, g_ref, bias_ref, o_ref, qs_ref, m_ref, l_ref, acc_ref = refs
    qi = pl.program_id(1)
    ki = pl.program_id(2)

    @pl.when(ki == 0)
    def _():
        qs_ref[...] = (q_ref[...] * (DH ** -0.5)).astype(BF16)
        m_ref[...] = jnp.full_like(m_ref, -jnp.inf)
        l_ref[...] = jnp.zeros_like(l_ref)
        acc_ref[...] = jnp.zeros_like(acc_ref)

    def body(diag):
        if mode == 'dsa':
            bias = bias_ref[0].astype(F32)
        elif diag:
            r = lax.broadcasted_iota(jnp.int32, (tq, tq), 0)
            c = lax.broadcasted_iota(jnp.int32, (tq, tq), 1)
            causal = r >= c
        for h in range(nh):
            hs = slice(h * DH, (h + 1) * DH)
            s = lax.dot_general(qs_ref[:, hs], k_ref[:, hs].astype(BF16), (((1,), (1,)), ((), ())),
                                preferred_element_type=F32)
            if mode == 'fox':
                s = s + (cq_ref[:, M_F + h:M_F + h + 1] - ck_ref[0, h:h + 1, :])
                if diag:
                    s = jnp.where(causal, s, NEG)
            else:
                s = s + bias
            m_prev = m_ref[h]
            m_new = jnp.maximum(m_prev, jnp.max(s, axis=1, keepdims=True))
            corr = jnp.exp(m_prev - m_new)
            p = jnp.exp(s - m_new)
            l_ref[h] = corr * l_ref[h] + jnp.sum(p, axis=1, keepdims=True)
            pv = jnp.dot(p.astype(BF16), v_ref[:, hs].astype(BF16), preferred_element_type=F32)
            acc_ref[:, hs] = corr * acc_ref[:, hs] + pv
            m_ref[h] = m_new

    @pl.when(ki < qi)
    def _():
        body(False)

    @pl.when(ki == qi)
    def _():
        body(True)
        for h in range(nh):
            hs = slice(h * DH, (h + 1) * DH)
            o_ref[:, hs] = acc_ref[:, hs] / l_ref[h] * _silu(g_ref[:, hs])


def _flash(z, extra, batch, seq, mode, tq=512):
    nh = HA if mode == 'fox' else HB
    wgrp = nh * DH
    nq = seq // tq
    cq, ck, cv, cg = ((Z_FA_Q, Z_FA_K, Z_FA_V, Z_FA_G) if mode == 'fox' else (Z_SB_Q, Z_SB_K, Z_SB_V, Z_SB_G))
    qmap = lambda col: (lambda b, i, k: (b * nq + i, col // wgrp))
    kmap = lambda col: (lambda b, i, k: (b * nq + jnp.minimum(k, i), col // wgrp))
    in_specs = [pl.BlockSpec((tq, wgrp), qmap(cq)), pl.BlockSpec((tq, wgrp), kmap(ck)),
                pl.BlockSpec((tq, wgrp), kmap(cv)), pl.BlockSpec((tq, wgrp), qmap(cg))]
    if mode == 'fox':
        c_rows, c_t = extra
        in_specs += [pl.BlockSpec((tq, LANE), lambda b, i, k: (b * nq + i, 0)),
                     pl.BlockSpec((1, 8, tq), lambda b, i, k: (b, 0, jnp.minimum(k, i)))]
        args = (z, z, z, z, c_rows, c_t)
    else:
        in_specs += [pl.BlockSpec((1, tq, tq), lambda b, i, k: (b, i, jnp.minimum(k, i)))]
        args = (z, z, z, z, extra)
    return pl.pallas_call(
        functools.partial(_flash_kernel, mode=mode, nh=nh, tq=tq),
        grid=(batch, nq, nq),
        in_specs=in_specs,
        out_specs=pl.BlockSpec((tq, wgrp), lambda b, i, k: (b * nq + i, 0)),
        out_shape=jax.ShapeDtypeStruct((batch * seq, wgrp), F32),
        scratch_shapes=[pltpu.VMEM((tq, wgrp), BF16), pltpu.VMEM((nh, tq, 1), F32),
                        pltpu.VMEM((nh, tq, 1), F32), pltpu.VMEM((tq, wgrp), F32)],
        compiler_params=_cparams(("parallel", "parallel", "arbitrary")),
    )(*args)


def _sort_key(x):
    b = pltpu.bitcast(x + 0.0, jnp.int32)
    return b ^ ((b >> 31) & 0x7FFFFFFF)


def _kth_threshold(count_ge, shape, k):
    def step(it, u):
        bit = jnp.left_shift(jnp.int32(1), 31 - it)
        cand = u | bit
        cnt = count_ge(cand ^ INT_MIN)
        return jnp.where(cnt >= k, cand, u)

    u = lax.fori_loop(0, 32, step, jnp.zeros(shape, jnp.int32))
    return u ^ INT_MIN


def _dsa_select_kernel(qi_ref, qm_ref, km_ref, bias_ref, key_ref, *, tq, n_sel):
    i = pl.program_id(1)
    seq = km_ref.shape[0]
    n_kc = seq // tq
    w = qm_ref[:, M_IXW:M_IXW + HI] * ((DI ** -0.5) * (HI ** -0.5))
    qb = qi_ref[...].astype(BF16)

    def score_chunk(c, _):
        c0 = pl.multiple_of(c * tq, tq)
        kc = km_ref[pl.ds(c0, tq), :][:, M_IXK:M_IXK + DI].astype(BF16)
        sc = jnp.zeros((tq, tq), F32)
        for h in range(HI):
            d = lax.dot_general(qb[:, h * DI:(h + 1) * DI], kc, (((1,), (1,)), ((), ())),
                                preferred_element_type=F32)
            sc = sc + jnp.maximum(d, 0.0) * w[:, h:h + 1]
        key = _sort_key(sc)
        r = lax.broadcasted_iota(jnp.int32, (tq, tq), 0)
        cc = lax.broadcasted_iota(jnp.int32, (tq, tq), 1)
        key = jnp.where((c < i) | (r >= cc), key, INT_MIN)
        key_ref[:, pl.ds(c0, tq)] = key
        return 0

    lax.fori_loop(0, i + 1, score_chunk, 0)

    def count_ge(t):
        def chunk(c, acc):
            c0 = pl.multiple_of(c * tq, tq)
            kk = key_ref[:, pl.ds(c0, tq)]
            for s in range(tq // LANE):
                acc = acc + jnp.where(kk[:, s * LANE:(s + 1) * LANE] >= t, 1.0, 0.0)
            return acc
        acc = lax.fori_loop(0, i + 1, chunk, jnp.zeros((tq, LANE), F32))
        return jnp.sum(acc, axis=1, keepdims=True)

    thr = _kth_threshold(count_ge, (tq, 1), float(n_sel))

    def write_chunk(c, _):
        c0 = pl.multiple_of(c * tq, tq)
        kk = key_ref[:, pl.ds(c0, tq)]
        sel = (kk >= thr) & (kk != INT_MIN)
        bias_ref[0, :, pl.ds(c0, tq)] = jnp.where(sel, 0.0, NEG).astype(BF16)
        return 0

    def fill_chunk(c, _):
        c0 = pl.multiple_of(c * tq, tq)
        bias_ref[0, :, pl.ds(c0, tq)] = jnp.full((tq, tq), NEG, BF16)
        return 0

    lax.fori_loop(0, i + 1, write_chunk, 0)
    lax.fori_loop(i + 1, n_kc, fill_chunk, 0)


def _dsa_select(z, batch, seq, tq=256):
    nq = seq // tq
    n_sel = min(TOPK, seq // 4)
    return pl.pallas_call(
        functools.partial(_dsa_select_kernel, tq=tq, n_sel=n_sel),
        grid=(batch, nq),
        in_specs=[pl.BlockSpec((tq, HI * DI), lambda b, i: (b * nq + i, Z_IX_Q // (HI * DI))),
                  pl.BlockSpec((tq, LANE), lambda b, i: (b * nq + i, Z_MISC // LANE)),
                  pl.BlockSpec((seq, LANE), lambda b, i: (b, Z_MISC // LANE))],
        out_specs=pl.BlockSpec((1, tq, seq), lambda b, i: (b, i, 0)),
        out_shape=jax.ShapeDtypeStruct((batch, seq, seq), BF16),
        scratch_shapes=[pltpu.VMEM((tq, seq), jnp.int32)],
        compiler_params=_cparams(("parallel", "parallel")),
    )(z, z, z)


def _dot_hi(a, b):
    return jnp.dot(a, b, precision=HIGHEST, preferred_element_type=F32)


def _rwkv_premix(cur, prev, p_ref, wb_ref, ab_ref, hones_ref):
    mix = lambda c, pv, mu: c + (pv - c) * mu
    xr = mix(cur['r'], prev['r'], p_ref[0:1, :])
    xk = mix(cur['k'], prev['k'], p_ref[1:2, :])
    xv = mix(cur['v'], prev['v'], p_ref[2:3, :])
    xwa = mix(cur['wa'], prev['wa'], p_ref[3:4, 0:LANE])
    w_raw = p_ref[4:5, :] + _dot_hi(jnp.tanh(xwa), wb_ref[...])
    nw = -w_raw
    softplus = jnp.maximum(nw, 0.0) + jnp.log(1.0 + jnp.exp(-jnp.abs(nw)))
    lw = -jnp.exp(-softplus - 0.5)
    a = 1.0 / (1.0 + jnp.exp(-(p_ref[5:6, :] + _dot_hi(xwa, ab_ref[...]))))
    kk0 = xk * p_ref[6:7, :]
    ss = _dot_hi(kk0 * kk0, hones_ref[...])
    kk = kk0 / jnp.maximum(jnp.sqrt(ss), 1e-12)
    k2 = xk * (1.0 + (a - 1.0) * p_ref[7:8, :])
    bonus = _dot_hi(xr * k2 * p_ref[8:9, :], hones_ref[...]) * xv
    return xr, k2, xv, lw, kk, a, bonus


def _rwkv_finish(out, bonus, g, p_ref, hones_ref):
    mean = _dot_hi(out, hones_ref[...]) * (1.0 / DC)
    d = out - mean
    var = _dot_hi(d * d, hones_ref[...]) * (1.0 / DC)
    y = d * lax.rsqrt(var + GN_EPS) * p_ref[9:10, :] + p_ref[10:11, :] + bonus
    return y * _silu(g)


def _rwkv_params(mu, w0, w_b, a0, a_b, k_k, k_a, r_k, ln_w, ln_b):
    pad = lambda v: jnp.concatenate([v, jnp.zeros((WC - v.shape[0],), F32)])
    rows = [mu[:WC], mu[WC:2 * WC], mu[2 * WC:3 * WC], pad(mu[3 * WC:]), w0, a0, k_k, k_a,
            r_k.reshape(WC), ln_w, ln_b]
    rows += [jnp.zeros((WC,), F32)] * (16 - len(rows))
    p = jnp.stack(rows, axis=0)
    wb = jnp.zeros((LANE, WC), F32).at[0:LORA].set(w_b)
    ab = jnp.zeros((LANE, WC), F32).at[LORA:2 * LORA].set(a_b)
    hid = jnp.arange(WC) // DC
    hones = (hid[:, None] == hid[None, :]).astype(F32)
    return p, wb, ab, hones


def _rwkv_chunk_kernel(r_ref, k_ref, v_ref, wa_ref, g_ref, p_ref, wb_ref, ab_ref, hones_ref,
                       o_ref, st_ref, s_ref, pr_ref, pk_ref, pv_ref, pwa_ref):
    step = pl.program_id(1)
    C = r_ref.shape[0]

    @pl.when(step == 0)
    def _():
        s_ref[...] = jnp.zeros_like(s_ref)
        pr_ref[...] = jnp.zeros_like(pr_ref)
        pk_ref[...] = jnp.zeros_like(pk_ref)
        pv_ref[...] = jnp.zeros_like(pv_ref)
        pwa_ref[...] = jnp.zeros_like(pwa_ref)

    def shifted(x, carry_ref):
        row = lax.broadcasted_iota(jnp.int32, x.shape, 0)
        pv = jnp.where(row == 0, carry_ref[...], pltpu.roll(x, 1, 0))
        carry_ref[...] = x[C - 1:C, :]
        return pv

    cur = {'r': r_ref[...], 'k': k_ref[...], 'v': v_ref[...], 'wa': wa_ref[...]}
    prev = {'r': shifted(cur['r'], pr_ref), 'k': shifted(cur['k'], pk_ref),
            'v': shifted(cur['v'], pv_ref), 'wa': shifted(cur['wa'], pwa_ref)}
    xr, k2, xv, lw, kk, a, bonus = _rwkv_premix(cur, prev, p_ref, wb_ref, ab_ref, hones_ref)

    ri = lax.broadcasted_iota(jnp.int32, (C, C), 0)
    ci = lax.broadcasted_iota(jnp.int32, (C, C), 1)
    low_incl = ri >= ci
    low_strict = ri > ci
    eye_c = jnp.where(ri == ci, 1.0, 0.0).astype(F32)
    lc = _dot_hi(jnp.where(low_incl, 1.0, 0.0).astype(F32), lw)
    lend = lc[C - 1:C, :]
    e_neg = jnp.exp(-lc)
    e_end = jnp.exp(lend - lc)
    bn = kk * a
    a_t = -kk * jnp.exp(lc - lw)
    r_t = xr * jnp.exp(lc)
    b_t = bn * e_neg
    k_t = k2 * e_neg
    b_h = bn * e_end
    k_h = k2 * e_end
    g_end = jnp.exp(lend)

    lane = lax.broadcasted_iota(jnp.int32, (1, LANE), 1)
    first = lane < DC
    pr_i = lax.broadcasted_iota(jnp.int32, (LANE, LANE), 0)
    pc_i = lax.broadcasted_iota(jnp.int32, (LANE, LANE), 1)
    same_head = (pr_i // DC) == (pc_i // DC)
    eye_p = pr_i == pc_i
    nt = (((1,), (1,)), ((), ()))
    tn = (((0,), (0,)), ((), ()))

    outs = []
    for p in range(HC // 2):
        ps = slice(p * LANE, (p + 1) * LANE)
        ar = jnp.concatenate([a_t[:, ps], r_t[:, ps]], axis=0)
        bk = jnp.concatenate([b_t[:, ps], k_t[:, ps]], axis=0)
        vp = xv[:, ps]
        s0 = s_ref[p]
        wst = _dot_hi(ar, s0)
        w1, o1 = wst[:C], wst[C:]
        rhs_h, t_h, pbk_h = [], [], []
        for hh in range(2):
            msk = first if hh == 0 else jnp.logical_not(first)
            x = lax.dot_general(jnp.where(msk, ar, 0.0), bk, nt, precision=HIGHEST,
                                preferred_element_type=F32)
            n_m = jnp.where(low_strict, x[:C, :C], 0.0)
            m_m = jnp.where(low_strict, x[:C, C:], 0.0)
            pb = jnp.where(low_incl, x[C:, :C], 0.0)
            pk = jnp.where(low_incl, x[C:, C:], 0.0)
            t_inv = eye_c + n_m
            pw = n_m
            sz = 2
            while sz < C:
                pw = _dot_hi(pw, pw)
                t_inv = t_inv + _dot_hi(pw, t_inv)
                sz *= 2
            rhs_h.append(w1 + _dot_hi(m_m, vp))
            t_h.append(t_inv)
            pbk_h.append(jnp.concatenate([pb, pk], axis=1))
        u = jnp.where(first, _dot_hi(t_h[0], rhs_h[0]), _dot_hi(t_h[1], rhs_h[1]))
        uv = jnp.concatenate([u, vp], axis=0)
        o = o1 + jnp.where(first, _dot_hi(pbk_h[0], uv), _dot_hi(pbk_h[1], uv))
        outs.append(o)
        bkh = jnp.concatenate([b_h[:, ps], k_h[:, ps]], axis=0)
        upd = lax.dot_general(bkh, uv, tn, precision=HIGHEST, preferred_element_type=F32)
        dg = jnp.where(eye_p, jnp.broadcast_to(g_end[:, ps], (LANE, LANE)), 0.0)
        s_ref[p] = _dot_hi(dg, s0) + jnp.where(same_head, upd, 0.0)

    out = jnp.concatenate(outs, axis=1)
    o_ref[...] = _rwkv_finish(out, bonus, g_ref[...], p_ref, hones_ref)

    @pl.when(step == pl.num_programs(1) - 1)
    def _():
        st_ref[0] = s_ref[...]


def _rwkv_prompt(z, params, batch, seq):
    p, wb, ab, hones = params
    C = RC_CHUNK
    nc = seq // C
    col = lambda off, w: (lambda b, i: (b * nc + i, off // w))
    const = lambda shp: pl.BlockSpec(shp, lambda b, i: (0,) * len(shp))
    o, st = pl.pallas_call(
        _rwkv_chunk_kernel,
        grid=(batch, nc),
        in_specs=[pl.BlockSpec((C, WC), col(Z_RC_R, WC)), pl.BlockSpec((C, WC), col(Z_RC_K, WC)),
                  pl.BlockSpec((C, WC), col(Z_RC_V, WC)), pl.BlockSpec((C, LANE), col(Z_RC_WA, LANE)),
                  pl.BlockSpec((C, WC), col(Z_RC_G, WC)),
                  const((16, WC)), const((LANE, WC)), const((LANE, WC)), const((WC, WC))],
        out_specs=[pl.BlockSpec((C, WC), lambda b, i: (b * nc + i, 0)),
                   pl.BlockSpec((1, HC // 2, LANE, LANE), lambda b, i: (b, 0, 0, 0))],
        out_shape=[jax.ShapeDtypeStruct((batch * seq, WC), F32),
                   jax.ShapeDtypeStruct((batch, HC // 2, LANE, LANE), F32)],
        scratch_shapes=[pltpu.VMEM((HC // 2, LANE, LANE), F32), pltpu.VMEM((1, WC), F32),
                        pltpu.VMEM((1, WC), F32), pltpu.VMEM((1, WC), F32), pltpu.VMEM((1, LANE), F32)],
        compiler_params=_cparams(("parallel", "arbitrary")),
    )(z, z, z, z, z, p, wb, ab, hones)
    st = st.reshape(batch, HC // 2, 2, DC, 2, DC)
    st = jnp.stack([st[:, :, 0, :, 0, :], st[:, :, 1, :, 1, :]], axis=2).reshape(batch, HC, DC, DC)
    return o, jnp.swapaxes(st, -1, -2)


def _paged_attn_kernel(*refs, mode, pps, nh):
    pt_ref = refs[0]
    q_ref, kn_ref, vn_ref, g_ref = refs[1:5]
    k_refs = refs[5:5 + pps]
    v_refs = refs[5 + pps:5 + 2 * pps]
    rest = refs[5 + 2 * pps:]
    if mode == 'fox':
        lf_refs = rest[:pps]
        lfn_ref = rest[pps]
        o_ref, qm_ref, m_ref, l_ref, acc_ref, carry_ref = rest[pps + 1:]
    else:
        bias_ref, biasn_ref = rest[:2]
        o_ref, qm_ref, m_ref, l_ref, acc_ref = rest[2:]
    del pt_ref
    step = pl.program_id(1)
    w = nh * DH
    hrow = lax.broadcasted_iota(jnp.int32, (8, w), 0)
    hcol = lax.broadcasted_iota(jnp.int32, (8, w), 1) // DH
    own = hrow == hcol

    @pl.when(step == 0)
    def _():
        qm_ref[...] = jnp.where(own, q_ref[0] * (DH ** -0.5), 0.0).astype(BF16)
        m_ref[...] = jnp.full_like(m_ref, -jnp.inf)
        l_ref[...] = jnp.zeros_like(l_ref)
        acc_ref[...] = jnp.zeros_like(acc_ref)
        if mode == 'fox':
            carry_ref[...] = jnp.zeros_like(carry_ref)

    nt = (((1,), (1,)), ((), ()))
    qm = qm_ref[...]
    s_parts = []
    if mode == 'fox':
        ur = lax.broadcasted_iota(jnp.int32, (LANE, LANE), 0)
        uc = lax.broadcasted_iota(jnp.int32, (LANE, LANE), 1)
        upper = jnp.where(ur <= uc, 1.0, 0.0).astype(F32)
        carry = carry_ref[...]
    for i in range(pps):
        s = lax.dot_general(qm, k_refs[i][...].astype(BF16), nt, preferred_element_type=F32)
        if mode == 'fox':
            cum = _dot_hi(lf_refs[i][...], upper) + carry
            carry = jnp.broadcast_to(cum[:, LANE - 1:LANE], (8, LANE))
            s = s - cum
        else:
            s = s + bias_ref[0, :, i * LANE:(i + 1) * LANE]
        s_parts.append(s)
    if mode == 'fox':
        carry_ref[...] = carry
    s_all = jnp.concatenate(s_parts, axis=1)
    m_prev = m_ref[...]
    m_new = jnp.maximum(m_prev, jnp.max(s_all, axis=1, keepdims=True))
    corr = jnp.exp(m_prev - m_new)
    p_all = jnp.exp(s_all - m_new)
    l_ref[...] = corr * l_ref[...] + jnp.sum(p_all, axis=1, keepdims=True)
    pv = jnp.zeros((8, w), F32)
    for i in range(pps):
        pv = pv + jnp.dot(p_all[:, i * LANE:(i + 1) * LANE].astype(BF16), v_refs[i][...].astype(BF16),
                          preferred_element_type=F32)
    acc_ref[...] = corr * acc_ref[...] + pv
    m_ref[...] = m_new

    @pl.when(step == pl.num_programs(1) - 1)
    def _():
        kn = kn_ref[0].astype(BF16).astype(F32)
        sn = jnp.sum(qm_ref[...].astype(F32) * kn, axis=1, keepdims=True)
        if mode == 'fox':
            sn = sn - (carry_ref[:, 0:1] + lfn_ref[0])
        else:
            sn = sn + biasn_ref[0, :, 0:1]
        m_p = m_ref[...]
        m_n = jnp.maximum(m_p, sn)
        cr = jnp.exp(m_p - m_n)
        pn = jnp.exp(sn - m_n)
        lsum = cr * l_ref[...] + pn
        acc = cr * acc_ref[...] + pn * vn_ref[0]
        o = jnp.sum(jnp.where(own, acc / lsum, 0.0), axis=0, keepdims=True)
        o_ref[0] = o * _silu(g_ref[0])


def _paged_attn(zs3, cache_k, cache_v, layer, page_table, extra, mode):
    db = zs3.shape[0]
    n_pages = page_table.shape[1]
    pps = min(PPS_ATTN, n_pages)
    nsteps = n_pages // pps
    nh = HA if mode == 'fox' else HB
    w = nh * DH
    cq, ck, cv, cg = ((Z_FA_Q, Z_FA_K, Z_FA_V, Z_FA_G) if mode == 'fox' else (Z_SB_Q, Z_SB_K, Z_SB_V, Z_SB_G))
    zspec = lambda col: pl.BlockSpec((1, 1, w), lambda b, s, pt: (b, 0, col // w))
    page_map = lambda i: (lambda b, s, pt: (layer, pt[b * n_pages + s * pps + i], 0, 0))
    in_specs = [zspec(cq), zspec(ck), zspec(cv), zspec(cg)]
    in_specs += [pl.BlockSpec((None, None, PAGE, w), page_map(i)) for i in range(pps)]
    in_specs += [pl.BlockSpec((None, None, PAGE, w), page_map(i)) for i in range(pps)]
    args = [zs3, zs3, zs3, zs3] + [cache_k] * pps + [cache_v] * pps
    scratch = [pltpu.VMEM((8, w), BF16), pltpu.VMEM((8, 1), F32), pltpu.VMEM((8, 1), F32),
               pltpu.VMEM((8, w), F32)]
    if mode == 'fox':
        logf_t, lf_new = extra
        in_specs += [pl.BlockSpec((None, None, 8, PAGE), page_map(i)) for i in range(pps)]
        in_specs += [pl.BlockSpec((1, 8, 1), lambda b, s, pt: (b, 0, 0))]
        args += [logf_t] * pps + [lf_new]
        scratch += [pltpu.VMEM((8, LANE), F32)]
    else:
        bias = extra
        in_specs += [pl.BlockSpec((1, 1, pps * LANE), lambda b, s, pt: (b, 0, s)),
                     pl.BlockSpec((1, 1, LANE), lambda b, s, pt: (b, 0, n_pages))]
        args += [bias, bias]
    grid_spec = pltpu.PrefetchScalarGridSpec(
        num_scalar_prefetch=1, grid=(db, nsteps), in_specs=in_specs,
        out_specs=pl.BlockSpec((1, 1, w), lambda b, s, pt: (b, 0, 0)),
        scratch_shapes=scratch)
    return pl.pallas_call(
        functools.partial(_paged_attn_kernel, mode=mode, pps=pps, nh=nh),
        grid_spec=grid_spec,
        out_shape=jax.ShapeDtypeStruct((db, 1, w), F32),
        compiler_params=_cparams(("parallel", "arbitrary")),
    )(page_table.reshape(-1), *args)


def _sample_select_kernel(*refs, pps, n_sel):
    pt_ref = refs[0]
    qi_ref, wi_ref, kin_ref = refs[1:4]
    ki_refs = refs[4:4 + pps]
    bias_ref, key_ref = refs[4 + pps:]
    del pt_ref
    step = pl.program_id(1)
    nsteps = pl.num_programs(1)
    past = key_ref.shape[1] - LANE
    nt = (((1,), (1,)), ((), ()))
    qb = qi_ref[0].astype(BF16)
    w = wi_ref[0] * ((DI ** -0.5) * (HI ** -0.5))

    def score(kmat):
        d = lax.dot_general(qb, kmat.astype(BF16), nt, preferred_element_type=F32)
        return jnp.sum(jnp.maximum(d, 0.0) * w, axis=0, keepdims=True)

    for i in range(pps):
        off = pl.multiple_of((step * pps + i) * LANE, LANE)
        key_ref[:, pl.ds(off, LANE)] = _sort_key(score(ki_refs[i][...]))

    @pl.when(step == nsteps - 1)
    def _():
        kn = jnp.broadcast_to(kin_ref[0], (8, DI))
        sn = score(kn)[:, 0:1]
        lane = lax.broadcasted_iota(jnp.int32, (1, LANE), 1)
        key_ref[:, past:past + LANE] = jnp.where(lane == 0, _sort_key(jnp.broadcast_to(sn, (1, LANE))), INT_MIN)
        keys = key_ref[...]

        def count_ge(t):
            return jnp.sum(jnp.where(keys >= t, 1.0, 0.0), axis=1, keepdims=True)

        thr = _kth_threshold(count_ge, (1, 1), float(n_sel))
        sel = (keys >= thr) & (keys != INT_MIN)
        bias_ref[0] = jnp.where(sel, 0.0, NEG)


def _sample_select(qi3, wi3, kin3, cache_ki, layer, page_table):
    db = qi3.shape[0]
    n_pages = page_table.shape[1]
    pps = min(PPS_IDX, n_pages)
    past = n_pages * PAGE
    n_sel = min(TOPK, (past + 1) // 4)
    page_map = lambda i: (lambda b, s, pt: (layer, pt[b * n_pages + s * pps + i], 0, 0))
    in_specs = [pl.BlockSpec((1, HI, DI), lambda b, s, pt: (b, 0, 0)),
                pl.BlockSpec((1, HI, 1), lambda b, s, pt: (b, 0, 0)),
                pl.BlockSpec((1, 1, DI), lambda b, s, pt: (b, 0, 0))]
    in_specs += [pl.BlockSpec((None, None, PAGE, DI), page_map(i)) for i in range(pps)]
    grid_spec = pltpu.PrefetchScalarGridSpec(
        num_scalar_prefetch=1, grid=(db, n_pages // pps), in_specs=in_specs,
        out_specs=pl.BlockSpec((1, 1, past + LANE), lambda b, s, pt: (b, 0, 0)),
        scratch_shapes=[pltpu.VMEM((1, past + LANE), jnp.int32)])
    return pl.pallas_call(
        functools.partial(_sample_select_kernel, pps=pps, n_sel=n_sel),
        grid_spec=grid_spec,
        out_shape=jax.ShapeDtypeStruct((db, 1, past + LANE), F32),
        compiler_params=_cparams(("parallel", "arbitrary")),
    )(page_table.reshape(-1), qi3, wi3, kin3, *([cache_ki] * pps))


def _rwkv_pre_kernel(r_ref, k_ref, v_ref, wa_ref, sr_ref, sk_ref, sv_ref, swa_ref, p_ref, wb_ref, ab_ref,
                     hones_ref, or_ref, ok_ref, ov_ref, ow_ref, onk_ref, oka_ref, ob_ref):
    cur = {'r': r_ref[...], 'k': k_ref[...], 'v': v_ref[...], 'wa': wa_ref[...]}
    prev = {'r': sr_ref[...], 'k': sk_ref[...], 'v': sv_ref[...], 'wa': swa_ref[...]}
    xr, k2, xv, lw, kk, a, bonus = _rwkv_premix(cur, prev, p_ref, wb_ref, ab_ref, hones_ref)
    or_ref[...] = xr
    ok_ref[...] = k2
    ov_ref[...] = xv
    ow_ref[...] = jnp.exp(lw)
    onk_ref[...] = -kk
    oka_ref[...] = kk * a
    ob_ref[...] = bonus


def _rwkv_step_kernel(s_ref, r_ref, k_ref, v_ref, w_ref, nk_ref, ka_ref, so_ref, o_ref):
    s = s_ref[...]
    sa = jnp.sum(s * nk_ref[...], axis=-1, keepdims=True)
    s2 = s * w_ref[...] + sa * ka_ref[...] + v_ref[...] * k_ref[...]
    so_ref[...] = s2
    o_ref[...] = jnp.sum(s2 * r_ref[...], axis=-1, keepdims=True)


def _rwkv_post_kernel(o_ref, b_ref, g_ref, p_ref, hones_ref, y_ref):
    y_ref[...] = _rwkv_finish(o_ref[...], b_ref[...], g_ref[...], p_ref, hones_ref)


def _rwkv_sample(zs, shift_prev, wkv, params, bb=8):
    p, wb, ab, hones = params
    db = zs.shape[0]
    full = lambda a: pl.BlockSpec(a.shape, lambda i: (0,) * a.ndim)
    zcol = lambda off, w: pl.BlockSpec((db, w), lambda i: (0, off // w))
    swa = jnp.concatenate([shift_prev[:, 3 * WC:], jnp.zeros((db, LANE - 2 * LORA), F32)], axis=1)
    sr, sk, sv = shift_prev[:, :WC], shift_prev[:, WC:2 * WC], shift_prev[:, 2 * WC:3 * WC]
    rows = pl.pallas_call(
        _rwkv_pre_kernel,
        grid=(1,),
        in_specs=[zcol(Z_RC_R, WC), zcol(Z_RC_K, WC), zcol(Z_RC_V, WC), zcol(Z_RC_WA, LANE),
                  full(sr), full(sk), full(sv), full(swa), full(p), full(wb), full(ab), full(hones)],
        out_specs=[pl.BlockSpec((db, WC), lambda i: (0, 0))] * 7,
        out_shape=[jax.ShapeDtypeStruct((db, WC), F32)] * 7,
        compiler_params=_cparams(("arbitrary",)),
    )(zs, zs, zs, zs, sr, sk, sv, swa, p, wb, ab, hones)
    xr, k2, xv, dec, nkk, kka, bonus = rows
    as_row = lambda t: t.reshape(db, HC, 1, DC)
    rspec = pl.BlockSpec((bb, HC, 1, DC), lambda i: (i, 0, 0, 0))
    cspec = pl.BlockSpec((bb, HC, DC, 1), lambda i: (i, 0, 0, 0))
    sspec = pl.BlockSpec((bb, HC, DC, DC), lambda i: (i, 0, 0, 0))
    s_new, o_col = pl.pallas_call(
        _rwkv_step_kernel,
        grid=(db // bb,),
        in_specs=[sspec, rspec, rspec, cspec, rspec, rspec, rspec],
        out_specs=[sspec, cspec],
        out_shape=[jax.ShapeDtypeStruct((db, HC, DC, DC), F32), jax.ShapeDtypeStruct((db, HC, DC, 1), F32)],
        compiler_params=_cparams(("parallel",)),
    )(wkv, as_row(xr), as_row(k2), xv.reshape(db, HC, DC, 1), as_row(dec), as_row(nkk), as_row(kka))
    o_rows = o_col.reshape(db, WC)
    y = pl.pallas_call(
        _rwkv_post_kernel,
        grid=(1,),
        in_specs=[full(o_rows), full(bonus), zcol(Z_RC_G, WC), full(p), full(hones)],
        out_specs=pl.BlockSpec((db, WC), lambda i: (0, 0)),
        out_shape=jax.ShapeDtypeStruct((db, WC), F32),
        compiler_params=_cparams(("arbitrary",)),
    )(o_rows, bonus, zs, p, hones)
    return y, s_new


def _shift_cols(z_last):
    return jnp.concatenate([z_last[..., Z_RC_R:Z_RC_R + 3 * WC], z_last[..., Z_RC_WA:Z_RC_WA + 2 * LORA]], axis=-1)


def _layer_prompt(x2d, batch, seq, gain, wz, bf_row, tabs, rw_params, wo_bf, fgain, final):
    z = _inproj(x2d, gain, wz, tabs, bf_row, tm=min(INPROJ_TM, seq))
    c_rows, c_t = _fox_cumsum(z, batch, seq, tm=min(CUM_TM, seq))
    o_a = _flash(z, (c_rows, c_t), batch, seq, 'fox', tq=min(FLASH_TQ, seq))
    bias = _dsa_select(z, batch, seq, tq=min(SEL_TQ, seq))
    o_b = _flash(z, bias, batch, seq, 'dsa', tq=min(FLASH_TQ, seq))
    o_c, wkv = _rwkv_prompt(z, rw_params, batch, seq)
    y = _outproj(o_a, o_b, o_c, x2d, wo_bf, fgain, tm=min(OUT_TM, seq), final=final)
    z3 = z.reshape(batch, seq, NZ)
    st = (z3[..., Z_FA_K:Z_FA_K + WA].reshape(batch, seq, HA, DH),
          z3[..., Z_FA_V:Z_FA_V + WA].reshape(batch, seq, HA, DH),
          z3[..., Z_MISC + M_F:Z_MISC + M_F + HA],
          z3[..., Z_SB_K:Z_SB_K + WB].reshape(batch, seq, HB, DH),
          z3[..., Z_SB_V:Z_SB_V + WB].reshape(batch, seq, HB, DH),
          z3[..., Z_MISC + M_IXK:Z_MISC + M_IXK + DI],
          wkv, _shift_cols(z3[:, -1]))
    return y, st


def _layer_sample(xs2d, layer, caches, page_table, gain, wz, bf_row, tabs, rw_params, wo_bf, fgain, final):
    c_fk, c_fv, c_flt, c_dk, c_dv, c_di, s_wkv, s_shift = caches
    db = xs2d.shape[0]
    zs = _inproj(xs2d, gain, wz, tabs, bf_row, tm=db)
    zs3 = zs.reshape(db, 1, NZ)
    misc = zs[:, Z_MISC:Z_MISC + LANE]
    lf_new = jnp.concatenate([misc[:, M_F:M_F + HA], jnp.zeros((db, 8 - HA), F32)], axis=1).reshape(db, 8, 1)
    o_a = _paged_attn(zs3, c_fk, c_fv, layer, page_table, (c_flt, lf_new), 'fox')
    qi3 = zs[:, Z_IX_Q:Z_IX_Q + HI * DI].reshape(db, HI, DI)
    wi3 = misc[:, M_IXW:M_IXW + HI].reshape(db, HI, 1)
    kin3 = misc[:, M_IXK:M_IXK + DI].reshape(db, 1, DI)
    bias = _sample_select(qi3, wi3, kin3, c_di, layer, page_table)
    o_b = _paged_attn(zs3, c_dk, c_dv, layer, page_table, bias, 'dsa')
    o_c, wkv = _rwkv_sample(zs, s_shift[layer], s_wkv[layer], rw_params)
    y = _outproj(o_a.reshape(db, WA), o_b.reshape(db, WB), o_c, xs2d, wo_bf, fgain, tm=db, final=final)
    st = (zs[:, Z_FA_K:Z_FA_K + WA].reshape(db, 1, HA, DH),
          zs[:, Z_FA_V:Z_FA_V + WA].reshape(db, 1, HA, DH),
          misc[:, M_F:M_F + HA].reshape(db, 1, HA),
          zs[:, Z_SB_K:Z_SB_K + WB].reshape(db, 1, HB, DH),
          zs[:, Z_SB_V:Z_SB_V + WB].reshape(db, 1, HB, DH),
          misc[:, M_IXK:M_IXK + DI].reshape(db, 1, DI),
          wkv, _shift_cols(zs))
    return y, st


def kernel(x_prompt, x_sample, cache_fox_k, cache_fox_v, cache_fox_logf, cache_dsa_k, cache_dsa_v, cache_dsa_kidx, state_rwkv_wkv, state_rwkv_shift, page_table, norm_gain, w_in, fox_forget_bias, rwkv_mu, rwkv_w0, rwkv_w_lora_b, rwkv_a0, rwkv_a_lora_b, rwkv_k_k, rwkv_k_a, rwkv_r_k, rwkv_ln_w, rwkv_ln_b, w_out, final_gain):
    batch, seq, d = x_prompt.shape
    db = x_sample.shape[0]
    assert x_sample.shape[1] == 1
    depth = w_in.shape[0]
    n_pool = cache_fox_k.shape[1]
    past = page_table.shape[1] * PAGE

    tabs_p = _rope_tables(jnp.arange(seq))
    tabs_s = _rope_tables(jnp.full((db,), past))
    c_fk = cache_fox_k.reshape(depth, n_pool, PAGE, WA)
    c_fv = cache_fox_v.reshape(depth, n_pool, PAGE, WA)
    c_dk = cache_dsa_k.reshape(depth, n_pool, PAGE, WB)
    c_dv = cache_dsa_v.reshape(depth, n_pool, PAGE, WB)
    c_flt = jnp.pad(jnp.swapaxes(cache_fox_logf, -1, -2), ((0, 0), (0, 0), (0, 8 - HA), (0, 0)))
    caches = (c_fk, c_fv, c_flt, c_dk, c_dv, cache_dsa_kidx, state_rwkv_wkv, state_rwkv_shift)

    xp = x_prompt.reshape(batch * seq, d)
    xs = x_sample.reshape(db, d)
    new_p, new_s = [], []
    for l in range(depth):
        final = l == depth - 1
        wz = _relayout_w_in(w_in[l])
        wo_bf = w_out[l].astype(BF16)
        bf_row = jnp.zeros((1, LANE), F32).at[0, M_F:M_F + HA].set(fox_forget_bias[l])
        rw_params = _rwkv_params(rwkv_mu[l], rwkv_w0[l], rwkv_w_lora_b[l], rwkv_a0[l], rwkv_a_lora_b[l],
                                 rwkv_k_k[l], rwkv_k_a[l], rwkv_r_k[l], rwkv_ln_w[l], rwkv_ln_b[l])
        xp, st_p = _layer_prompt(xp, batch, seq, norm_gain[l], wz, bf_row, tabs_p, rw_params, wo_bf,
                                 final_gain, final)
        xs, st_s = _layer_sample(xs, l, caches, page_table, norm_gain[l], wz, bf_row, tabs_s, rw_params,
                                 wo_bf, final_gain, final)
        new_p.append(st_p)
        new_s.append(st_s)
    outs_p = [jnp.stack(a) for a in zip(*new_p)]
    outs_s = [jnp.stack(a) for a in zip(*new_s)]
    return (xp.reshape(batch, seq, d), xs.reshape(db, 1, d), *outs_p, *outs_s)
```

```python
import functools

import jax
import jax.numpy as jnp
from jax import lax
from jax.experimental import pallas as pl
from jax.experimental.pallas import tpu as pltpu

F32 = jnp.float32
BF16 = jnp.bfloat16
HIGHEST = lax.Precision.HIGHEST

DH = 128
HA = 6
HB = 6
WA = HA * DH
WB = HB * DH
DC = 64
HC = 8
WC = HC * DC
HI = 16
DI = 64
TOPK = 256
ROPE_THETA = 500000.0
ROT_DIM = DH // 4
ROT_DIM_IDX = DI // 4
LORA = 32
PAGE = 128
RMS_EPS = 1e-6
GN_EPS = 64e-5
C_SHIFT = 3 * WC + 2 * LORA
LANE = 128
NEG = -1e30
INT_MIN = -(2 ** 31)
VMEM_LIMIT = 56 * 1024 * 1024

Z_FA_Q, Z_FA_K, Z_FA_V, Z_FA_G = 0, 768, 1536, 2304
Z_SB_V, Z_SB_G, Z_SB_Q, Z_SB_K = 3072, 3840, 4608, 5376
Z_IX_Q = 6144
Z_RC_G, Z_RC_R, Z_RC_K, Z_RC_V = 7168, 7680, 8192, 8704
Z_RC_WA = 9216
Z_MISC = 9344
NZ = 9728
M_IXK, M_IXW, M_F = 0, 64, 80
TN = 512
TILE_KINDS = ((0, 9, 0), (9, 12, 1), (12, 14, 2), (14, 18, 0), (18, 19, 3))
RC_CHUNK = 64
RC_ROWS = 128
INPROJ_TM = 512
OUT_TM = 256
CUM_TM = 256
FLASH_TQ = 512
SEL_TQ = 256
PPS_ATTN = 8
PPS_IDX = 32


def _cparams(sem):
    return pltpu.CompilerParams(dimension_semantics=sem, vmem_limit_bytes=VMEM_LIMIT)


def _relayout_w_in(w):
    d = w.shape[0]
    o = {}
    off = 0
    for name, width in (('fa_q', WA), ('fa_k', WA), ('fa_v', WA), ('fa_f', HA), ('fa_g', WA),
                        ('sb_q', WB), ('sb_k', WB), ('sb_v', WB), ('sb_g', WB),
                        ('ix_q', HI * DI), ('ix_k', DI), ('ix_w', HI),
                        ('rc_shift', C_SHIFT), ('rc_g', WC)):
        o[name] = w[:, off:off + width]
        off += width
    zeros = lambda n: jnp.zeros((d, n), w.dtype)
    rs = o['rc_shift']
    parts = [o['fa_q'], o['fa_k'], o['fa_v'], o['fa_g'], o['sb_v'], o['sb_g'], o['sb_q'], o['sb_k'],
             o['ix_q'], o['rc_g'], rs[:, :3 * WC], rs[:, 3 * WC:], zeros(64),
             o['ix_k'], o['ix_w'], o['fa_f'], zeros(LANE - DI - HI - HA), zeros(NZ - Z_MISC - LANE)]
    wz = jnp.concatenate(parts, axis=1)
    assert wz.shape[1] == NZ
    return wz.astype(BF16)


def _rope_tables(pos):
    lane = jnp.arange(LANE)
    posf = pos.astype(F32)[:, None]

    def tab(head_w, rot, lane_limit):
        half = rot // 2
        lh = lane % head_w
        inv = ROPE_THETA ** (-(lh % half).astype(F32) * (2.0 / rot))
        ang = posf * inv[None, :]
        c, s = jnp.cos(ang), jnp.sin(ang)
        act = (lh < rot) & (lane < lane_limit)
        cos_t = jnp.where(act[None, :], c, 1.0)
        sin_lo = jnp.where((act & (lh >= half))[None, :], s, 0.0)
        sin_hi = jnp.where((act & (lh < half))[None, :], -s, 0.0)
        return [cos_t, sin_lo, sin_hi]

    return jnp.stack(tab(DH, ROT_DIM, LANE) + tab(DI, ROT_DIM_IDX, LANE) + tab(DI, ROT_DIM_IDX, DI), axis=0)


def _log_sigmoid(x):
    return jnp.minimum(x, 0.0) - jnp.log(1.0 + jnp.exp(-jnp.abs(x)))


def _rope_chunk(x, tab_ref, kind, half):
    c = tab_ref[3 * kind + 0]
    s_lo = tab_ref[3 * kind + 1]
    s_hi = tab_ref[3 * kind + 2]
    return x * c + pltpu.roll(x, half, 1) * s_lo + pltpu.roll(x, LANE - half, 1) * s_hi


def _inproj_kernel(x_ref, g_ref, w_ref, tab_ref, bf_ref, z_ref, xn_ref):
    j = pl.program_id(1)

    @pl.when(j == 0)
    def _():
        x = x_ref[...]
        ms = jnp.mean(x * x, axis=-1, keepdims=True)
        xn_ref[...] = (x * lax.rsqrt(ms + RMS_EPS) * g_ref[...]).astype(BF16)

    acc = jnp.dot(xn_ref[...], w_ref[...], preferred_element_type=F32)
    n_chunk = TN // LANE
    for lo, hi, kind in TILE_KINDS:
        @pl.when((j >= lo) & (j < hi))
        def _(kind=kind):
            if kind == 0:
                z_ref[...] = acc
            elif kind in (1, 2):
                half = ROT_DIM // 2 if kind == 1 else ROT_DIM_IDX // 2
                for c in range(n_chunk):
                    sl = slice(c * LANE, (c + 1) * LANE)
                    z_ref[:, sl] = _rope_chunk(acc[:, sl], tab_ref, kind - 1, half)
            else:
                z_ref[...] = acc
                c = (Z_MISC - Z_RC_WA) // LANE
                sl = slice(c * LANE, (c + 1) * LANE)
                xm = acc[:, sl]
                roped = _rope_chunk(xm, tab_ref, 2, ROT_DIM_IDX // 2)
                lane = lax.broadcasted_iota(jnp.int32, xm.shape, 1)
                is_f = (lane >= M_F) & (lane < M_F + HA)
                z_ref[:, sl] = jnp.where(is_f, _log_sigmoid(xm + bf_ref[...]), roped)


def _inproj(x2d, gain, wz, tabs, bf_row, tm):
    rows, d = x2d.shape
    n_pos_blk = tabs.shape[1] // tm
    grid = (rows // tm, NZ // TN)
    return pl.pallas_call(
        _inproj_kernel,
        grid=grid,
        in_specs=[
            pl.BlockSpec((tm, d), lambda i, j: (i, 0)),
            pl.BlockSpec((1, d), lambda i, j: (0, 0)),
            pl.BlockSpec((d, TN), lambda i, j: (0, j)),
            pl.BlockSpec((9, tm, LANE), lambda i, j: (0, i % n_pos_blk, 0)),
            pl.BlockSpec((1, LANE), lambda i, j: (0, 0)),
        ],
        out_specs=pl.BlockSpec((tm, TN), lambda i, j: (i, j)),
        out_shape=jax.ShapeDtypeStruct((rows, NZ), F32),
        scratch_shapes=[pltpu.VMEM((tm, d), BF16)],
        compiler_params=_cparams(("parallel", "arbitrary")),
    )(x2d, gain.reshape(1, d), wz, tabs, bf_row)


def _outproj_kernel(oa_ref, ob_ref, oc_ref, x_ref, w_ref, fg_ref, y_ref, *, final):
    acc = jnp.dot(oa_ref[...].astype(BF16), w_ref[0:WA, :], preferred_element_type=F32)
    acc += jnp.dot(ob_ref[...].astype(BF16), w_ref[WA:WA + WB, :], preferred_element_type=F32)
    acc += jnp.dot(oc_ref[...].astype(BF16), w_ref[WA + WB:, :], preferred_element_type=F32)
    y = x_ref[...] + acc
    if final:
        ms = jnp.mean(y * y, axis=-1, keepdims=True)
        y = y * lax.rsqrt(ms + RMS_EPS) * fg_ref[...]
    y_ref[...] = y


def _outproj(oa, ob, oc, x2d, w_bf, fgain, tm, final):
    rows, d = x2d.shape
    row_spec = lambda w: pl.BlockSpec((tm, w), lambda i: (i, 0))
    return pl.pallas_call(
        functools.partial(_outproj_kernel, final=final),
        grid=(rows // tm,),
        in_specs=[row_spec(WA), row_spec(WB), row_spec(WC), row_spec(d),
                  pl.BlockSpec((WA + WB + WC, d), lambda i: (0, 0)),
                  pl.BlockSpec((1, d), lambda i: (0, 0))],
        out_specs=row_spec(d),
        out_shape=jax.ShapeDtypeStruct((rows, d), F32),
        compiler_params=_cparams(("parallel",)),
    )(oa, ob, oc, x2d, w_bf, fgain.reshape(1, d))


def _cumsum_kernel(z_ref, c_ref, ct_ref, carry_ref):
    @pl.when(pl.program_id(1) == 0)
    def _():
        carry_ref[...] = jnp.zeros_like(carry_ref)

    lf = z_ref[...]
    tm = lf.shape[0]
    r = lax.broadcasted_iota(jnp.int32, (tm, tm), 0)
    c = lax.broadcasted_iota(jnp.int32, (tm, tm), 1)
    tri = jnp.where(r >= c, 1.0, 0.0).astype(F32)
    cum = jnp.dot(tri, lf, precision=HIGHEST, preferred_element_type=F32) + carry_ref[...]
    carry_ref[...] = cum[tm - 1:tm, :]
    c_ref[...] = cum
    ct_ref[0] = cum.T[M_F:M_F + 8, :]


def _fox_cumsum(z, batch, seq, tm=256):
    nblk = seq // tm
    return pl.pallas_call(
        _cumsum_kernel,
        grid=(batch, nblk),
        in_specs=[pl.BlockSpec((tm, LANE), lambda b, i: (b * nblk + i, Z_MISC // LANE))],
        out_specs=[pl.BlockSpec((tm, LANE), lambda b, i: (b * nblk + i, 0)),
                   pl.BlockSpec((1, 8, tm), lambda b, i: (b, 0, i))],
        out_shape=[jax.ShapeDtypeStruct((batch * seq, LANE), F32),
                   jax.ShapeDtypeStruct((batch, 8, seq), F32)],
        scratch_shapes=[pltpu.VMEM((1, LANE), F32)],
        compiler_params=_cparams(("parallel", "arbitrary")),
    )(z)


def _silu(g):
    return g / (1.0 + jnp.exp(-g))


def _flash_kernel(*refs, mode, nh, tq):
    if mode == 'fox':
        q_ref, k_ref, v_ref, g_ref, cq_ref, ck_ref, o_ref, qs_ref, m_ref, l_ref, acc_ref = refs
    else:
        q_ref, k_ref, v_ref, g_ref, bias_ref, o_ref, qs_ref, m_ref, l_ref, acc_ref = refs
    qi = pl.program_id(1)
    ki = pl.program_id(2)

    @pl.when(ki == 0)
    def _():
        qs_ref[...] = (q_ref[...] * (DH ** -0.5)).astype(BF16)
        m_ref[...] = jnp.full_like(m_ref, -jnp.inf)
        l_ref[...] = jnp.zeros_like(l_ref)
        acc_ref[...] = jnp.zeros_like(acc_ref)

    def body(diag):
        if mode == 'dsa':
            bias = bias_ref[0].astype(F32)
        elif diag:
            r = lax.broadcasted_iota(jnp.int32, (tq, tq), 0)
            c = lax.broadcasted_iota(jnp.int32, (tq, tq), 1)
            causal = r >= c
        for h in range(nh):
            hs = slice(h * DH, (h + 1) * DH)
            s = lax.dot_general(qs_ref[:, hs], k_ref[:, hs].astype(BF16), (((1,), (1,)), ((), ())),
                                preferred_element_type=F32)
            if mode == 'fox':
                s = s + (cq_ref[:, M_F + h:M_F + h + 1] - ck_ref[0, h:h + 1, :])
                if diag:
                    s = jnp.where(causal, s, NEG)
            else:
                s = s + bias
            m_prev = m_ref[h]
            m_new = jnp.maximum(m_prev, jnp.max(s, axis=1, keepdims=True))
            corr = jnp.exp(m_prev - m_new)
            p = jnp.exp(s - m_new)
            l_ref[h] = corr * l_ref[h] + jnp.sum(p, axis=1, keepdims=True)
            pv = jnp.dot(p.astype(BF16), v_ref[:, hs].astype(BF16), preferred_element_type=F32)
            acc_ref[:, hs] = corr * acc_ref[:, hs] + pv
            m_ref[h] = m_new

    @pl.when(ki < qi)
    def _():
        body(False)

    @pl.when(ki == qi)
    def _():
        body(True)
        for h in range(nh):
            hs = slice(h * DH, (h + 1) * DH)
            o_ref[:, hs] = acc_ref[:, hs] / l_ref[h] * _silu(g_ref[:, hs])


def _flash(z, extra, batch, seq, mode, tq=512):
    nh = HA if mode == 'fox' else HB
    wgrp = nh * DH
    nq = seq // tq
    cq, ck, cv, cg = ((Z_FA_Q, Z_FA_K, Z_FA_V, Z_FA_G) if mode == 'fox' else (Z_SB_Q, Z_SB_K, Z_SB_V, Z_SB_G))
    qmap = lambda col: (lambda b, i, k: (b * nq + i, col // wgrp))
    kmap = lambda col: (lambda b, i, k: (b * nq + jnp.minimum(k, i), col // wgrp))
    in_specs = [pl.BlockSpec((tq, wgrp), qmap(cq)), pl.BlockSpec((tq, wgrp), kmap(ck)),
                pl.BlockSpec((tq, wgrp), kmap(cv)), pl.BlockSpec((tq, wgrp), qmap(cg))]
    if mode == 'fox':
        c_rows, c_t = extra
        in_specs += [pl.BlockSpec((tq, LANE), lambda b, i, k: (b * nq + i, 0)),
                     pl.BlockSpec((1, 8, tq), lambda b, i, k: (b, 0, jnp.minimum(k, i)))]
        args = (z, z, z, z, c_rows, c_t)
    else:
        in_specs += [pl.BlockSpec((1, tq, tq), lambda b, i, k: (b, i, jnp.minimum(k, i)))]
        args = (z, z, z, z, extra)
    return pl.pallas_call(
        functools.partial(_flash_kernel, mode=mode, nh=nh, tq=tq),
        grid=(batch, nq, nq),
        in_specs=in_specs,
        out_specs=pl.BlockSpec((tq, wgrp), lambda b, i, k: (b * nq + i, 0)),
        out_shape=jax.ShapeDtypeStruct((batch * seq, wgrp), F32),
        scratch_shapes=[pltpu.VMEM((tq, wgrp), BF16), pltpu.VMEM((nh, tq, 1), F32),
                        pltpu.VMEM((nh, tq, 1), F32), pltpu.VMEM((tq, wgrp), F32)],
        compiler_params=_cparams(("parallel", "parallel", "arbitrary")),
    )(*args)


def _sort_key(x):
    b = pltpu.bitcast(x + 0.0, jnp.int32)
    return b ^ ((b >> 31) & 0x7FFFFFFF)


def _kth_threshold(count_ge, shape, k):
    def step(it, u):
        bit = jnp.left_shift(jnp.int32(1), 31 - it)
        cand = u | bit
        cnt = count_ge(cand ^ INT_MIN)
        return jnp.where(cnt >= k, cand, u)

    u = lax.fori_loop(0, 32, step, jnp.zeros(shape, jnp.int32))
    return u ^ INT_MIN


def _dsa_select_kernel(qi_ref, qm_ref, km_ref, bias_ref, key_ref, qst_ref, wb_ref, *, tq, n_sel):
    i = pl.program_id(1)
    seq = km_ref.shape[0]
    n_kc = seq // tq
    half = tq // 2
    nslab = tq // LANE
    nt = (((1,), (1,)), ((), ()))
    w = qm_ref[:, M_IXW:M_IXW + HI] * ((DI ** -0.5) * (HI ** -0.5))
    for h in range(HI):
        qst_ref[h] = qi_ref[:, h * DI:(h + 1) * DI].astype(BF16)
        wb_ref[h] = jnp.broadcast_to(w[:, h:h + 1], (tq, LANE))

    def score_chunk(c, _):
        c0 = pl.multiple_of(c * tq, tq)
        kc = km_ref[pl.ds(c0, tq), :][:, M_IXK:M_IXK + DI].astype(BF16)
        for rh in range(2):
            rows = slice(rh * half, (rh + 1) * half)
            sc = jnp.zeros((half, tq), F32)
            for h in range(HI):
                d = lax.dot_general(qst_ref[h, rows, :], kc, nt, preferred_element_type=F32)
                wt = jnp.concatenate([wb_ref[h, rows, :]] * nslab, axis=1)
                sc = sc + jnp.maximum(d, 0.0) * wt
            key = _sort_key(sc)
            r = lax.broadcasted_iota(jnp.int32, (half, tq), 0) + rh * half
            cc = lax.broadcasted_iota(jnp.int32, (half, tq), 1)
            key_ref[rows, pl.ds(c0, tq)] = jnp.where((c < i) | (r >= cc), key, INT_MIN)
        return 0

    lax.fori_loop(0, i + 1, score_chunk, 0)

    def count_ge(t):
        cnts = []
        for rh in range(2):
            rows = slice(rh * half, (rh + 1) * half)
            tt = jnp.broadcast_to(t[rows], (half, LANE))

            def chunk(c, acc, rows=rows, tt=tt):
                c0 = pl.multiple_of(c * tq, tq)
                kk = key_ref[rows, pl.ds(c0, tq)]
                for s in range(nslab):
                    acc = acc + jnp.where(kk[:, s * LANE:(s + 1) * LANE] >= tt, 1.0, 0.0)
                return acc

            acc = lax.fori_loop(0, i + 1, chunk, jnp.zeros((half, LANE), F32))
            cnts.append(jnp.sum(acc, axis=1, keepdims=True))
        return jnp.concatenate(cnts, axis=0)

    thr = _kth_threshold(count_ge, (tq, 1), float(n_sel))

    def write_chunk(c, _):
        c0 = pl.multiple_of(c * tq, tq)
        kk = key_ref[:, pl.ds(c0, tq)]
        sel = (kk >= thr) & (kk != INT_MIN)
        bias_ref[0, :, pl.ds(c0, tq)] = jnp.where(sel, 0.0, NEG).astype(BF16)
        return 0

    def fill_chunk(c, _):
        c0 = pl.multiple_of(c * tq, tq)
        bias_ref[0, :, pl.ds(c0, tq)] = jnp.full((tq, tq), NEG, BF16)
        return 0

    lax.fori_loop(0, i + 1, write_chunk, 0)
    lax.fori_loop(i + 1, n_kc, fill_chunk, 0)


def _dsa_select(z, batch, seq, tq=256):
    nq = seq // tq
    n_sel = min(TOPK, seq // 4)
    return pl.pallas_call(
        functools.partial(_dsa_select_kernel, tq=tq, n_sel=n_sel),
        grid=(batch, nq),
        in_specs=[pl.BlockSpec((tq, HI * DI), lambda b, i: (b * nq + i, Z_IX_Q // (HI * DI))),
                  pl.BlockSpec((tq, LANE), lambda b, i: (b * nq + i, Z_MISC // LANE)),
                  pl.BlockSpec((seq, LANE), lambda b, i: (b, Z_MISC // LANE))],
        out_specs=pl.BlockSpec((1, tq, seq), lambda b, i: (b, i, 0)),
        out_shape=jax.ShapeDtypeStruct((batch, seq, seq), BF16),
        scratch_shapes=[pltpu.VMEM((tq, seq), jnp.int32), pltpu.VMEM((HI, tq, DI), BF16),
                        pltpu.VMEM((HI, tq, LANE), F32)],
        compiler_params=_cparams(("parallel", "parallel")),
    )(z, z, z)


DIMS_NN = (((1,), (0,)), ((), ()))
DIMS_NT = (((1,), (1,)), ((), ()))
DIMS_TN = (((0,), (0,)), ((), ()))


def _dot_hi(a, b):
    return jnp.dot(a, b, precision=HIGHEST, preferred_element_type=F32)


def _split(x):
    hi = x.astype(BF16)
    lo = (x - hi.astype(F32)).astype(BF16)
    return hi, lo


def _dot3(a, b, dims=DIMS_NN):
    ah, al = a if isinstance(a, tuple) else _split(a)
    bh, bl = b if isinstance(b, tuple) else _split(b)
    d = lambda x, y: lax.dot_general(x, y, dims, preferred_element_type=F32)
    return d(ah, bh) + (d(ah, bl) + d(al, bh))


def _dot2(x, m_bf):
    hi, lo = _split(x)
    return (jnp.dot(hi, m_bf, preferred_element_type=F32) + jnp.dot(lo, m_bf, preferred_element_type=F32))


def _dot2l(m_bf, x):
    hi, lo = _split(x)
    return (jnp.dot(m_bf, hi, preferred_element_type=F32) + jnp.dot(m_bf, lo, preferred_element_type=F32))


def _rwkv_premix(cur, prev, p_ref, wb_ref, ab_ref, hones_ref):
    mix = lambda c, pv, mu: c + (pv - c) * mu
    xr = mix(cur['r'], prev['r'], p_ref[0:1, :])
    xk = mix(cur['k'], prev['k'], p_ref[1:2, :])
    xv = mix(cur['v'], prev['v'], p_ref[2:3, :])
    xwa = mix(cur['wa'], prev['wa'], p_ref[3:4, 0:LANE])
    hones = hones_ref[...]
    w_raw = p_ref[4:5, :] + _dot3(jnp.tanh(xwa), (wb_ref[0], wb_ref[1]))
    nw = -w_raw
    softplus = jnp.maximum(nw, 0.0) + jnp.log(1.0 + jnp.exp(-jnp.abs(nw)))
    lw = -jnp.exp(-softplus - 0.5)
    a = 1.0 / (1.0 + jnp.exp(-(p_ref[5:6, :] + _dot3(xwa, (ab_ref[0], ab_ref[1])))))
    kk0 = xk * p_ref[6:7, :]
    ss = _dot2(kk0 * kk0, hones)
    kk = kk0 / jnp.maximum(jnp.sqrt(ss), 1e-12)
    k2 = xk * (1.0 + (a - 1.0) * p_ref[7:8, :])
    bonus = _dot2(xr * k2 * p_ref[8:9, :], hones) * xv
    return xr, k2, xv, lw, kk, a, bonus


def _rwkv_finish(out, bonus, g, p_ref, hones_ref):
    hones = hones_ref[...]
    mean = _dot2(out, hones) * (1.0 / DC)
    d = out - mean
    var = _dot2(d * d, hones) * (1.0 / DC)
    y = d * lax.rsqrt(var + GN_EPS) * p_ref[9:10, :] + p_ref[10:11, :] + bonus
    return y * _silu(g)


def _rwkv_params(mu, w0, w_b, a0, a_b, k_k, k_a, r_k, ln_w, ln_b):
    pad = lambda v: jnp.concatenate([v, jnp.zeros((WC - v.shape[0],), F32)])
    rows = [mu[:WC], mu[WC:2 * WC], mu[2 * WC:3 * WC], pad(mu[3 * WC:]), w0, a0, k_k, k_a,
            r_k.reshape(WC), ln_w, ln_b]
    rows += [jnp.zeros((WC,), F32)] * (16 - len(rows))
    p = jnp.stack(rows, axis=0)
    split2 = lambda m: jnp.stack(_split(m), axis=0)
    wb = split2(jnp.zeros((LANE, WC), F32).at[0:LORA].set(w_b))
    ab = split2(jnp.zeros((LANE, WC), F32).at[LORA:2 * LORA].set(a_b))
    hid = jnp.arange(WC) // DC
    hones = (hid[:, None] == hid[None, :]).astype(BF16)
    return p, wb, ab, hones


def _rwkv_chunk_terms(units, first, strict, incl, eye):
    C = units[0][0].shape[0]
    cat0 = lambda xs: jnp.concatenate(xs, axis=0)
    cat1 = lambda xs: jnp.concatenate(xs, axis=1)
    stack = lambda x: cat0([jnp.where(first, x, 0.0), jnp.where(first, 0.0, x)])
    dup = lambda x: cat0([x, x])
    la = [stack(u[0]) for u in units]
    lr = [stack(u[1]) for u in units]
    vs = [stack(u[6]) for u in units]
    xx = [_dot3(cat0([a, r]), cat0([dup(u[2]), dup(u[3])]), DIMS_NT) for a, r, u in zip(la, lr, units)]
    n_m = [jnp.where(strict, x[:2 * C, :2 * C], 0.0) for x in xx]
    m_m = [jnp.where(strict, x[:2 * C, 2 * C:], 0.0) for x in xx]
    pb = [jnp.where(incl, x[2 * C:, :2 * C], 0.0) for x in xx]
    pk = [jnp.where(incl, x[2 * C:, 2 * C:], 0.0) for x in xx]
    t_inv = [jnp.where(eye, 1.0, 0.0) + n for n in n_m]
    pw_s = [_split(n) for n in n_m]
    sz = 2
    while sz < C:
        pw_s = [_split(_dot3(s, s)) for s in pw_s]
        t_inv = [t + _dot3(s, t) for s, t in zip(pw_s, t_inv)]
        sz *= 2
    mv = [_dot3(m, v) for m, v in zip(m_m, vs)]
    ta = [_dot3(t, cat1([a, m])) for t, a, m in zip(t_inv, la, mv)]
    big = [_split(cat0([t, cat1([jnp.zeros_like(v), v])])) for t, v in zip(ta, vs)]
    pp = [_dot3(cat1([b, k]), g) for b, k, g in zip(pb, pk, big)]
    php = [_dot3(cat0([stack(u[4]), stack(u[5])]), g, DIMS_TN) for u, g in zip(units, big)]
    terms = []
    for u, r, p, h in zip(units, lr, pp, php):
        r1s = r + p[:, :LANE]
        ovs = p[:, LANE:]
        phi = jnp.where(eye, jnp.broadcast_to(u[7], (LANE, LANE)), 0.0) + h[:, :LANE]
        terms.append((cat0([phi, r1s[:C] + r1s[C:]]), ovs[:C] + ovs[C:], h[:, LANE:]))
    return terms


def _rwkv_chunk_kernel(r_ref, k_ref, v_ref, wa_ref, g_ref, p_ref, wb_ref, ab_ref, hones_ref,
                       o_ref, st_ref, s_ref, pr_ref, pk_ref, pv_ref, pwa_ref):
    step = pl.program_id(1)
    R = r_ref.shape[0]
    C = RC_CHUNK
    ncs = R // C

    @pl.when(step == 0)
    def _():
        s_ref[...] = jnp.zeros_like(s_ref)
        pr_ref[...] = jnp.zeros_like(pr_ref)
        pk_ref[...] = jnp.zeros_like(pk_ref)
        pv_ref[...] = jnp.zeros_like(pv_ref)
        pwa_ref[...] = jnp.zeros_like(pwa_ref)

    def shifted(x, carry_ref):
        row = lax.broadcasted_iota(jnp.int32, x.shape, 0)
        pv = jnp.where(row == 0, carry_ref[...], pltpu.roll(x, 1, 0))
        carry_ref[...] = x[R - 1:R, :]
        return pv

    cur = {'r': r_ref[...], 'k': k_ref[...], 'v': v_ref[...], 'wa': wa_ref[...]}
    prev = {'r': shifted(cur['r'], pr_ref), 'k': shifted(cur['k'], pk_ref),
            'v': shifted(cur['v'], pv_ref), 'wa': shifted(cur['wa'], pwa_ref)}
    xr, k2, xv, lw, kk, a, bonus = _rwkv_premix(cur, prev, p_ref, wb_ref, ab_ref, hones_ref)

    ri = lax.broadcasted_iota(jnp.int32, (R, R), 0)
    ci = lax.broadcasted_iota(jnp.int32, (R, R), 1)
    tril = jnp.where((ri >= ci) & ((ri // C) == (ci // C)), 1.0, 0.0).astype(BF16)
    lc = _dot2l(tril, lw)
    lend = jnp.concatenate([jnp.broadcast_to(lc[(c + 1) * C - 1:(c + 1) * C, :], (C, WC)) for c in range(ncs)],
                           axis=0)
    e_neg = jnp.exp(-lc)
    e_end = jnp.exp(lend - lc)
    bn = kk * a
    a_t = -kk * jnp.exp(lc - lw)
    r_t = xr * jnp.exp(lc)
    b_t = bn * e_neg
    k_t = k2 * e_neg
    b_h = bn * e_end
    k_h = k2 * e_end

    first = lax.broadcasted_iota(jnp.int32, (1, LANE), 1) < DC
    pr_i = lax.broadcasted_iota(jnp.int32, (LANE, LANE), 0)
    pc_i = lax.broadcasted_iota(jnp.int32, (LANE, LANE), 1)
    same = (pr_i // DC) == (pc_i // DC)
    strict = same & (pr_i > pc_i)
    incl = same & (pr_i >= pc_i)
    eye = pr_i == pc_i

    units = []
    for c in range(ncs):
        rs = slice(c * C, (c + 1) * C)
        g_row = jnp.exp(lc[(c + 1) * C - 1:(c + 1) * C, :])
        for p in range(HC // 2):
            ps = slice(p * LANE, (p + 1) * LANE)
            units.append((a_t[rs, ps], r_t[rs, ps], b_t[rs, ps], k_t[rs, ps], b_h[rs, ps], k_h[rs, ps],
                          xv[rs, ps], g_row[:, ps]))
    terms = _rwkv_chunk_terms(units, first, strict, incl, eye)

    states = [s_ref[p] for p in range(HC // 2)]
    out_rows = []
    for c in range(ncs):
        outs = []
        for p in range(HC // 2):
            lhs, ov, psi = terms[c * (HC // 2) + p]
            sr = _dot3(lhs, states[p])
            outs.append(sr[LANE:] + ov)
            states[p] = sr[:LANE] + psi
        out_rows.append(jnp.concatenate(outs, axis=1))
    out = jnp.concatenate(out_rows, axis=0)
    o_ref[...] = _rwkv_finish(out, bonus, g_ref[...], p_ref, hones_ref)
    for p in range(HC // 2):
        s_ref[p] = states[p]

    @pl.when(step == pl.num_programs(1) - 1)
    def _():
        for p in range(HC // 2):
            st_ref[0, p] = states[p]


def _rwkv_prompt(z, params, batch, seq):
    p, wb, ab, hones = params
    R = min(RC_ROWS, seq)
    nc = seq // R
    col = lambda off, w: (lambda b, i: (b * nc + i, off // w))
    const = lambda shp: pl.BlockSpec(shp, lambda b, i: (0,) * len(shp))
    o, st = pl.pallas_call(
        _rwkv_chunk_kernel,
        grid=(batch, nc),
        in_specs=[pl.BlockSpec((R, WC), col(Z_RC_R, WC)), pl.BlockSpec((R, WC), col(Z_RC_K, WC)),
                  pl.BlockSpec((R, WC), col(Z_RC_V, WC)), pl.BlockSpec((R, LANE), col(Z_RC_WA, LANE)),
                  pl.BlockSpec((R, WC), col(Z_RC_G, WC)),
                  const((16, WC)), const((2, LANE, WC)), const((2, LANE, WC)), const((WC, WC))],
        out_specs=[pl.BlockSpec((R, WC), lambda b, i: (b * nc + i, 0)),
                   pl.BlockSpec((1, HC // 2, LANE, LANE), lambda b, i: (b, 0, 0, 0))],
        out_shape=[jax.ShapeDtypeStruct((batch * seq, WC), F32),
                   jax.ShapeDtypeStruct((batch, HC // 2, LANE, LANE), F32)],
        scratch_shapes=[pltpu.VMEM((HC // 2, LANE, LANE), F32), pltpu.VMEM((1, WC), F32),
                        pltpu.VMEM((1, WC), F32), pltpu.VMEM((1, WC), F32), pltpu.VMEM((1, LANE), F32)],
        compiler_params=_cparams(("parallel", "arbitrary")),
    )(z, z, z, z, z, p, wb, ab, hones)
    st = st.reshape(batch, HC // 2, 2, DC, 2, DC)
    st = jnp.stack([st[:, :, 0, :, 0, :], st[:, :, 1, :, 1, :]], axis=2).reshape(batch, HC, DC, DC)
    return o, jnp.swapaxes(st, -1, -2)


def _paged_attn_kernel(*refs, mode, pps, nh):
    pt_ref = refs[0]
    q_ref, kn_ref, vn_ref, g_ref = refs[1:5]
    k_refs = refs[5:5 + pps]
    v_refs = refs[5 + pps:5 + 2 * pps]
    rest = refs[5 + 2 * pps:]
    if mode == 'fox':
        lf_refs = rest[:pps]
        lfn_ref = rest[pps]
        o_ref, qm_ref, m_ref, l_ref, acc_ref, carry_ref = rest[pps + 1:]
    else:
        bias_ref, biasn_ref = rest[:2]
        o_ref, qm_ref, m_ref, l_ref, acc_ref = rest[2:]
    del pt_ref
    step = pl.program_id(1)
    w = nh * DH
    hrow = lax.broadcasted_iota(jnp.int32, (8, w), 0)
    hcol = lax.broadcasted_iota(jnp.int32, (8, w), 1) // DH
    own = hrow == hcol

    @pl.when(step == 0)
    def _():
        qm_ref[...] = jnp.where(own, q_ref[0] * (DH ** -0.5), 0.0).astype(BF16)
        m_ref[...] = jnp.full_like(m_ref, -jnp.inf)
        l_ref[...] = jnp.zeros_like(l_ref)
        acc_ref[...] = jnp.zeros_like(acc_ref)
        if mode == 'fox':
            carry_ref[...] = jnp.zeros_like(carry_ref)

    nt = (((1,), (1,)), ((), ()))
    qm = qm_ref[...]
    s_parts = []
    if mode == 'fox':
        ur = lax.broadcasted_iota(jnp.int32, (LANE, LANE), 0)
        uc = lax.broadcasted_iota(jnp.int32, (LANE, LANE), 1)
        upper = jnp.where(ur <= uc, 1.0, 0.0).astype(F32)
        carry = carry_ref[...]
    cat = lambda ref: jnp.concatenate([ref[h] for h in range(nh)], axis=1).astype(BF16)
    for i in range(pps):
        s = lax.dot_general(qm, cat(k_refs[i]), nt, preferred_element_type=F32)
        if mode == 'fox':
            cum = _dot_hi(lf_refs[i][...], upper) + carry
            carry = jnp.broadcast_to(cum[:, LANE - 1:LANE], (8, LANE))
            s = s - cum
        else:
            s = s + bias_ref[0, :, i * LANE:(i + 1) * LANE]
        s_parts.append(s)
    if mode == 'fox':
        carry_ref[...] = carry
    s_all = jnp.concatenate(s_parts, axis=1)
    m_prev = m_ref[...]
    m_new = jnp.maximum(m_prev, jnp.max(s_all, axis=1, keepdims=True))
    corr = jnp.exp(m_prev - m_new)
    p_all = jnp.exp(s_all - m_new)
    l_ref[...] = corr * l_ref[...] + jnp.sum(p_all, axis=1, keepdims=True)
    pv = jnp.zeros((8, w), F32)
    for i in range(pps):
        pv = pv + jnp.dot(p_all[:, i * LANE:(i + 1) * LANE].astype(BF16), cat(v_refs[i]),
                          preferred_element_type=F32)
    acc_ref[...] = corr * acc_ref[...] + pv
    m_ref[...] = m_new

    @pl.when(step == pl.num_programs(1) - 1)
    def _():
        kn = kn_ref[0].astype(BF16).astype(F32)
        sn = jnp.sum(qm_ref[...].astype(F32) * kn, axis=1, keepdims=True)
        if mode == 'fox':
            sn = sn - (carry_ref[:, 0:1] + lfn_ref[0])
        else:
            sn = sn + biasn_ref[0, :, 0:1]
        m_p = m_ref[...]
        m_n = jnp.maximum(m_p, sn)
        cr = jnp.exp(m_p - m_n)
        pn = jnp.exp(sn - m_n)
        lsum = cr * l_ref[...] + pn
        acc = cr * acc_ref[...] + pn * vn_ref[0]
        o = jnp.sum(jnp.where(own, acc / lsum, 0.0), axis=0, keepdims=True)
        o_ref[0] = o * _silu(g_ref[0])


def _paged_attn(zs3, cache_k, cache_v, layer, page_table, extra, mode):
    db = zs3.shape[0]
    n_pages = page_table.shape[1]
    pps = min(PPS_ATTN, n_pages)
    nsteps = n_pages // pps
    nh = HA if mode == 'fox' else HB
    w = nh * DH
    cq, ck, cv, cg = ((Z_FA_Q, Z_FA_K, Z_FA_V, Z_FA_G) if mode == 'fox' else (Z_SB_Q, Z_SB_K, Z_SB_V, Z_SB_G))
    zspec = lambda col: pl.BlockSpec((1, 1, w), lambda b, s, pt: (b, 0, col // w))
    page_map = lambda i: (lambda b, s, pt: (layer, pt[b * n_pages + s * pps + i], 0, 0))
    page_map5 = lambda i: (lambda b, s, pt: (layer, pt[b * n_pages + s * pps + i], 0, 0, 0))
    in_specs = [zspec(cq), zspec(ck), zspec(cv), zspec(cg)]
    in_specs += [pl.BlockSpec((None, None, nh, PAGE, DH), page_map5(i)) for i in range(pps)]
    in_specs += [pl.BlockSpec((None, None, nh, PAGE, DH), page_map5(i)) for i in range(pps)]
    args = [zs3, zs3, zs3, zs3] + [cache_k] * pps + [cache_v] * pps
    scratch = [pltpu.VMEM((8, w), BF16), pltpu.VMEM((8, 1), F32), pltpu.VMEM((8, 1), F32),
               pltpu.VMEM((8, w), F32)]
    if mode == 'fox':
        logf_t, lf_new = extra
        in_specs += [pl.BlockSpec((None, None, 8, PAGE), page_map(i)) for i in range(pps)]
        in_specs += [pl.BlockSpec((1, 8, 1), lambda b, s, pt: (b, 0, 0))]
        args += [logf_t] * pps + [lf_new]
        scratch += [pltpu.VMEM((8, LANE), F32)]
    else:
        bias = extra
        in_specs += [pl.BlockSpec((1, 1, pps * LANE), lambda b, s, pt: (b, 0, s)),
                     pl.BlockSpec((1, 1, LANE), lambda b, s, pt: (b, 0, n_pages))]
        args += [bias, bias]
    grid_spec = pltpu.PrefetchScalarGridSpec(
        num_scalar_prefetch=1, grid=(db, nsteps), in_specs=in_specs,
        out_specs=pl.BlockSpec((1, 1, w), lambda b, s, pt: (b, 0, 0)),
        scratch_shapes=scratch)
    return pl.pallas_call(
        functools.partial(_paged_attn_kernel, mode=mode, pps=pps, nh=nh),
        grid_spec=grid_spec,
        out_shape=jax.ShapeDtypeStruct((db, 1, w), F32),
        compiler_params=_cparams(("parallel", "arbitrary")),
    )(page_table.reshape(-1), *args)


def _sample_select_kernel(*refs, pps, n_sel):
    pt_ref = refs[0]
    qi_ref, wi_ref, kin_ref = refs[1:4]
    ki_refs = refs[4:4 + pps]
    bias_ref, key_ref = refs[4 + pps:]
    del pt_ref
    step = pl.program_id(1)
    nsteps = pl.num_programs(1)
    past = key_ref.shape[1] - LANE
    qb = qi_ref[0].astype(BF16)
    w = wi_ref[0] * ((DI ** -0.5) * (HI ** -0.5))

    def score(kmat_t):
        d = jnp.dot(qb, kmat_t.astype(BF16), preferred_element_type=F32)
        return jnp.sum(jnp.maximum(d, 0.0) * w, axis=0, keepdims=True)

    for i in range(pps):
        off = pl.multiple_of((step * pps + i) * LANE, LANE)
        key_ref[:, pl.ds(off, LANE)] = _sort_key(score(ki_refs[i][...]))

    @pl.when(step == nsteps - 1)
    def _():
        dn = jnp.sum(qb.astype(F32) * kin_ref[0].astype(BF16).astype(F32), axis=1, keepdims=True)
        sn = jnp.sum(jnp.maximum(dn, 0.0) * w, axis=0, keepdims=True)
        lane = lax.broadcasted_iota(jnp.int32, (1, LANE), 1)
        key_ref[:, past:past + LANE] = jnp.where(lane == 0, _sort_key(jnp.broadcast_to(sn, (1, LANE))), INT_MIN)
        keys = key_ref[...]

        def count_ge(t):
            return jnp.sum(jnp.where(keys >= t, 1.0, 0.0), axis=1, keepdims=True)

        thr = _kth_threshold(count_ge, (1, 1), float(n_sel))
        sel = (keys >= thr) & (keys != INT_MIN)
        bias_ref[0] = jnp.where(sel, 0.0, NEG)


def _sample_select(qi3, wi3, kin3, cache_ki, layer, page_table):
    db = qi3.shape[0]
    n_pages = page_table.shape[1]
    pps = min(PPS_IDX, n_pages)
    past = n_pages * PAGE
    n_sel = min(TOPK, (past + 1) // 4)
    page_map = lambda i: (lambda b, s, pt: (layer, pt[b * n_pages + s * pps + i], 0, 0))
    in_specs = [pl.BlockSpec((1, HI, DI), lambda b, s, pt: (b, 0, 0)),
                pl.BlockSpec((1, HI, 1), lambda b, s, pt: (b, 0, 0)),
                pl.BlockSpec((1, 1, DI), lambda b, s, pt: (b, 0, 0))]
    in_specs += [pl.BlockSpec((None, None, DI, PAGE), page_map(i)) for i in range(pps)]
    grid_spec = pltpu.PrefetchScalarGridSpec(
        num_scalar_prefetch=1, grid=(db, n_pages // pps), in_specs=in_specs,
        out_specs=pl.BlockSpec((1, 1, past + LANE), lambda b, s, pt: (b, 0, 0)),
        scratch_shapes=[pltpu.VMEM((1, past + LANE), jnp.int32)])
    return pl.pallas_call(
        functools.partial(_sample_select_kernel, pps=pps, n_sel=n_sel),
        grid_spec=grid_spec,
        out_shape=jax.ShapeDtypeStruct((db, 1, past + LANE), F32),
        compiler_params=_cparams(("parallel", "arbitrary")),
    )(page_table.reshape(-1), qi3, wi3, kin3, *([cache_ki] * pps))


def _rwkv_pre_kernel(r_ref, k_ref, v_ref, wa_ref, sr_ref, sk_ref, sv_ref, swa_ref, p_ref, wb_ref, ab_ref,
                     hones_ref, or_ref, ok_ref, ov_ref, ow_ref, onk_ref, oka_ref, ob_ref):
    cur = {'r': r_ref[...], 'k': k_ref[...], 'v': v_ref[...], 'wa': wa_ref[...]}
    prev = {'r': sr_ref[...], 'k': sk_ref[...], 'v': sv_ref[...], 'wa': swa_ref[...]}
    xr, k2, xv, lw, kk, a, bonus = _rwkv_premix(cur, prev, p_ref, wb_ref, ab_ref, hones_ref)
    or_ref[...] = xr
    ok_ref[...] = k2
    ov_ref[...] = xv
    ow_ref[...] = jnp.exp(lw)
    onk_ref[...] = -kk
    oka_ref[...] = kk * a
    ob_ref[...] = bonus


def _rwkv_step_kernel(s_ref, r_ref, k_ref, v_ref, w_ref, nk_ref, ka_ref, so_ref, o_ref):
    s = s_ref[...]
    sa = jnp.sum(s * nk_ref[...], axis=-1, keepdims=True)
    s2 = s * w_ref[...] + sa * ka_ref[...] + v_ref[...] * k_ref[...]
    so_ref[...] = s2
    o_ref[...] = jnp.sum(s2 * r_ref[...], axis=-1, keepdims=True)


def _rwkv_post_kernel(o_ref, b_ref, g_ref, p_ref, hones_ref, y_ref):
    y_ref[...] = _rwkv_finish(o_ref[...], b_ref[...], g_ref[...], p_ref, hones_ref)


def _rwkv_sample(zs, shift_prev, wkv, params, bb=8):
    p, wb, ab, hones = params
    db = zs.shape[0]
    full = lambda a: pl.BlockSpec(a.shape, lambda i: (0,) * a.ndim)
    zcol = lambda off, w: pl.BlockSpec((db, w), lambda i: (0, off // w))
    swa = jnp.concatenate([shift_prev[:, 3 * WC:], jnp.zeros((db, LANE - 2 * LORA), F32)], axis=1)
    sr, sk, sv = shift_prev[:, :WC], shift_prev[:, WC:2 * WC], shift_prev[:, 2 * WC:3 * WC]
    rows = pl.pallas_call(
        _rwkv_pre_kernel,
        grid=(1,),
        in_specs=[zcol(Z_RC_R, WC), zcol(Z_RC_K, WC), zcol(Z_RC_V, WC), zcol(Z_RC_WA, LANE),
                  full(sr), full(sk), full(sv), full(swa), full(p), full(wb), full(ab), full(hones)],
        out_specs=[pl.BlockSpec((db, WC), lambda i: (0, 0))] * 7,
        out_shape=[jax.ShapeDtypeStruct((db, WC), F32)] * 7,
        compiler_params=_cparams(("arbitrary",)),
    )(zs, zs, zs, zs, sr, sk, sv, swa, p, wb, ab, hones)
    xr, k2, xv, dec, nkk, kka, bonus = rows
    as_row = lambda t: t.reshape(db, HC, 1, DC)
    rspec = pl.BlockSpec((bb, HC, 1, DC), lambda i: (i, 0, 0, 0))
    cspec = pl.BlockSpec((bb, HC, DC, 1), lambda i: (i, 0, 0, 0))
    sspec = pl.BlockSpec((bb, HC, DC, DC), lambda i: (i, 0, 0, 0))
    s_new, o_col = pl.pallas_call(
        _rwkv_step_kernel,
        grid=(db // bb,),
        in_specs=[sspec, rspec, rspec, cspec, rspec, rspec, rspec],
        out_specs=[sspec, cspec],
        out_shape=[jax.ShapeDtypeStruct((db, HC, DC, DC), F32), jax.ShapeDtypeStruct((db, HC, DC, 1), F32)],
        compiler_params=_cparams(("parallel",)),
    )(wkv, as_row(xr), as_row(k2), xv.reshape(db, HC, DC, 1), as_row(dec), as_row(nkk), as_row(kka))
    o_rows = o_col.reshape(db, WC)
    y = pl.pallas_call(
        _rwkv_post_kernel,
        grid=(1,),
        in_specs=[full(o_rows), full(bonus), zcol(Z_RC_G, WC), full(p), full(hones)],
        out_specs=pl.BlockSpec((db, WC), lambda i: (0, 0)),
        out_shape=jax.ShapeDtypeStruct((db, WC), F32),
        compiler_params=_cparams(("arbitrary",)),
    )(o_rows, bonus, zs, p, hones)
    return y, s_new


def _shift_cols(z_last):
    return jnp.concatenate([z_last[..., Z_RC_R:Z_RC_R + 3 * WC], z_last[..., Z_RC_WA:Z_RC_WA + 2 * LORA]], axis=-1)


def _layer_prompt(x2d, batch, seq, gain, wz, bf_row, tabs, rw_params, wo_bf, fgain, final):
    z = _inproj(x2d, gain, wz, tabs, bf_row, tm=min(INPROJ_TM, seq))
    c_rows, c_t = _fox_cumsum(z, batch, seq, tm=min(CUM_TM, seq))
    o_a = _flash(z, (c_rows, c_t), batch, seq, 'fox', tq=min(FLASH_TQ, seq))
    bias = _dsa_select(z, batch, seq, tq=min(SEL_TQ, seq))
    o_b = _flash(z, bias, batch, seq, 'dsa', tq=min(FLASH_TQ, seq))
    o_c, wkv = _rwkv_prompt(z, rw_params, batch, seq)
    y = _outproj(o_a, o_b, o_c, x2d, wo_bf, fgain, tm=min(OUT_TM, seq), final=final)
    z3 = z.reshape(batch, seq, NZ)
    st = (z3[..., Z_FA_K:Z_FA_K + WA].reshape(batch, seq, HA, DH),
          z3[..., Z_FA_V:Z_FA_V + WA].reshape(batch, seq, HA, DH),
          z3[..., Z_MISC + M_F:Z_MISC + M_F + HA],
          z3[..., Z_SB_K:Z_SB_K + WB].reshape(batch, seq, HB, DH),
          z3[..., Z_SB_V:Z_SB_V + WB].reshape(batch, seq, HB, DH),
          z3[..., Z_MISC + M_IXK:Z_MISC + M_IXK + DI],
          wkv, _shift_cols(z3[:, -1]))
    return y, st


def _layer_sample(xs2d, layer, caches, page_table, gain, wz, bf_row, tabs, rw_params, wo_bf, fgain, final):
    c_fk, c_fv, c_flt, c_dk, c_dv, c_di, s_wkv, s_shift = caches
    db = xs2d.shape[0]
    zs = _inproj(xs2d, gain, wz, tabs, bf_row, tm=db)
    zs3 = zs.reshape(db, 1, NZ)
    misc = zs[:, Z_MISC:Z_MISC + LANE]
    lf_new = jnp.concatenate([misc[:, M_F:M_F + HA], jnp.zeros((db, 8 - HA), F32)], axis=1).reshape(db, 8, 1)
    o_a = _paged_attn(zs3, c_fk, c_fv, layer, page_table, (c_flt, lf_new), 'fox')
    qi3 = zs[:, Z_IX_Q:Z_IX_Q + HI * DI].reshape(db, HI, DI)
    wi3 = misc[:, M_IXW:M_IXW + HI].reshape(db, HI, 1)
    kin3 = misc[:, M_IXK:M_IXK + DI].reshape(db, 1, DI)
    bias = _sample_select(qi3, wi3, kin3, c_di, layer, page_table)
    o_b = _paged_attn(zs3, c_dk, c_dv, layer, page_table, bias, 'dsa')
    o_c, wkv = _rwkv_sample(zs, s_shift[layer], s_wkv[layer], rw_params)
    y = _outproj(o_a.reshape(db, WA), o_b.reshape(db, WB), o_c, xs2d, wo_bf, fgain, tm=db, final=final)
    st = (zs[:, Z_FA_K:Z_FA_K + WA].reshape(db, 1, HA, DH),
          zs[:, Z_FA_V:Z_FA_V + WA].reshape(db, 1, HA, DH),
          misc[:, M_F:M_F + HA].reshape(db, 1, HA),
          zs[:, Z_SB_K:Z_SB_K + WB].reshape(db, 1, HB, DH),
          zs[:, Z_SB_V:Z_SB_V + WB].reshape(db, 1, HB, DH),
          misc[:, M_IXK:M_IXK + DI].reshape(db, 1, DI),
          wkv, _shift_cols(zs))
    return y, st


def kernel(x_prompt, x_sample, cache_fox_k, cache_fox_v, cache_fox_logf, cache_dsa_k, cache_dsa_v, cache_dsa_kidx, state_rwkv_wkv, state_rwkv_shift, page_table, norm_gain, w_in, fox_forget_bias, rwkv_mu, rwkv_w0, rwkv_w_lora_b, rwkv_a0, rwkv_a_lora_b, rwkv_k_k, rwkv_k_a, rwkv_r_k, rwkv_ln_w, rwkv_ln_b, w_out, final_gain):
    batch, seq, d = x_prompt.shape
    db = x_sample.shape[0]
    assert x_sample.shape[1] == 1
    depth = w_in.shape[0]
    past = page_table.shape[1] * PAGE

    tabs_p = _rope_tables(jnp.arange(seq))
    tabs_s = _rope_tables(jnp.full((db,), past))
    head_major = lambda c: jnp.transpose(c, (0, 1, 3, 2, 4))
    c_flt = jnp.pad(jnp.swapaxes(cache_fox_logf, -1, -2), ((0, 0), (0, 0), (0, 8 - HA), (0, 0)))
    caches = (head_major(cache_fox_k), head_major(cache_fox_v), c_flt, head_major(cache_dsa_k),
              head_major(cache_dsa_v), jnp.swapaxes(cache_dsa_kidx, -1, -2), state_rwkv_wkv, state_rwkv_shift)

    xp = x_prompt.reshape(batch * seq, d)
    xs = x_sample.reshape(db, d)
    new_p, new_s = [], []
    for l in range(depth):
        final = l == depth - 1
        wz = _relayout_w_in(w_in[l])
        wo_bf = w_out[l].astype(BF16)
        bf_row = jnp.zeros((1, LANE), F32).at[0, M_F:M_F + HA].set(fox_forget_bias[l])
        rw_params = _rwkv_params(rwkv_mu[l], rwkv_w0[l], rwkv_w_lora_b[l], rwkv_a0[l], rwkv_a_lora_b[l],
                                 rwkv_k_k[l], rwkv_k_a[l], rwkv_r_k[l], rwkv_ln_w[l], rwkv_ln_b[l])
        xp, st_p = _layer_prompt(xp, batch, seq, norm_gain[l], wz, bf_row, tabs_p, rw_params, wo_bf,
                                 final_gain, final)
        xs, st_s = _layer_sample(xs, l, caches, page_table, norm_gain[l], wz, bf_row, tabs_s, rw_params,
                                 wo_bf, final_gain, final)
        new_p.append(st_p)
        new_s.append(st_s)
    outs_p = [jnp.stack(a) for a in zip(*new_p)]
    outs_s = [jnp.stack(a) for a in zip(*new_s)]
    return (xp.reshape(batch, seq, d), xs.reshape(db, 1, d), *outs_p, *outs_s)
```

```python
import functools

import jax
import jax.numpy as jnp
from jax import lax
from jax.experimental import pallas as pl
from jax.experimental.pallas import tpu as pltpu

F32 = jnp.float32
BF16 = jnp.bfloat16
HIGHEST = lax.Precision.HIGHEST

DH = 128
HA = 6
HB = 6
WA = HA * DH
WB = HB * DH
DC = 64
HC = 8
WC = HC * DC
HI = 16
DI = 64
TOPK = 256
ROPE_THETA = 500000.0
ROT_DIM = DH // 4
ROT_DIM_IDX = DI // 4
LORA = 32
PAGE = 128
RMS_EPS = 1e-6
GN_EPS = 64e-5
C_SHIFT = 3 * WC + 2 * LORA
LANE = 128
NEG = -1e30
INT_MIN = -(2 ** 31)
VMEM_LIMIT = 56 * 1024 * 1024

Z_FA_Q, Z_FA_K, Z_FA_V, Z_FA_G = 0, 768, 1536, 2304
Z_SB_V, Z_SB_G, Z_SB_Q, Z_SB_K = 3072, 3840, 4608, 5376
Z_IX_Q = 6144
Z_RC_G, Z_RC_R, Z_RC_K, Z_RC_V = 7168, 7680, 8192, 8704
Z_RC_WA = 9216
Z_MISC = 9344
NZ = 9728
M_IXK, M_IXW, M_F = 0, 64, 80
TN = 512
TILE_KINDS = ((0, 9, 0), (9, 12, 1), (12, 14, 2), (14, 18, 0), (18, 19, 3))
RC_CHUNK = 64
RC_ROWS = 128
INPROJ_TM = 1024
OUT_TM = 256
CUM_TM = 256
FLASH_TQ = 512
FLASH_RB = 128
FLASH_KB = 256
SEL_TQ = 256
PPS_ATTN = 8
PPS_IDX = 32


def _cparams(sem):
    return pltpu.CompilerParams(dimension_semantics=sem, vmem_limit_bytes=VMEM_LIMIT)


def _relayout_w_in(w):
    d = w.shape[0]
    wt = w.T
    o = {}
    off = 0
    for name, width in (('fa_q', WA), ('fa_k', WA), ('fa_v', WA), ('fa_f', HA), ('fa_g', WA),
                        ('sb_q', WB), ('sb_k', WB), ('sb_v', WB), ('sb_g', WB),
                        ('ix_q', HI * DI), ('ix_k', DI), ('ix_w', HI),
                        ('rc_shift', C_SHIFT), ('rc_g', WC)):
        o[name] = wt[off:off + width]
        off += width
    zeros = lambda n: jnp.zeros((n, d), w.dtype)
    rs = o['rc_shift']
    parts = [o['fa_q'], o['fa_k'], o['fa_v'], o['fa_g'], o['sb_v'], o['sb_g'], o['sb_q'], o['sb_k'],
             o['ix_q'], o['rc_g'], rs[:3 * WC], rs[3 * WC:], zeros(64),
             o['ix_k'], o['ix_w'], o['fa_f'], zeros(LANE - DI - HI - HA), zeros(NZ - Z_MISC - LANE)]
    wz = jnp.concatenate(parts, axis=0)
    assert wz.shape[0] == NZ
    return wz.astype(BF16)


def _rope_tables(pos):
    lane = jnp.arange(LANE)
    posf = pos.astype(F32)[:, None]

    def tab(head_w, rot, lane_limit):
        half = rot // 2
        lh = lane % head_w
        inv = ROPE_THETA ** (-(lh % half).astype(F32) * (2.0 / rot))
        ang = posf * inv[None, :]
        c, s = jnp.cos(ang), jnp.sin(ang)
        act = (lh < rot) & (lane < lane_limit)
        cos_t = jnp.where(act[None, :], c, 1.0)
        sin_lo = jnp.where((act & (lh >= half))[None, :], s, 0.0)
        sin_hi = jnp.where((act & (lh < half))[None, :], -s, 0.0)
        return [cos_t, sin_lo, sin_hi]

    return jnp.stack(tab(DH, ROT_DIM, LANE) + tab(DI, ROT_DIM_IDX, LANE) + tab(DI, ROT_DIM_IDX, DI), axis=0)


def _log_sigmoid(x):
    return jnp.minimum(x, 0.0) - jnp.log(1.0 + jnp.exp(-jnp.abs(x)))


def _rope_chunk(x, tab_ref, kind, half):
    c = tab_ref[3 * kind + 0]
    s_lo = tab_ref[3 * kind + 1]
    s_hi = tab_ref[3 * kind + 2]
    return x * c + pltpu.roll(x, half, 1) * s_lo + pltpu.roll(x, LANE - half, 1) * s_hi


def _inproj_kernel(x_ref, g_ref, w_ref, tab_ref, bf_ref, z_ref, xn_ref):
    j = pl.program_id(1)

    @pl.when(j == 0)
    def _():
        x = x_ref[...]
        ms = jnp.mean(x * x, axis=-1, keepdims=True)
        xn_ref[...] = (x * lax.rsqrt(ms + RMS_EPS) * g_ref[...]).astype(BF16)

    acc = lax.dot_general(xn_ref[...], w_ref[...], (((1,), (1,)), ((), ())), preferred_element_type=F32)
    n_chunk = TN // LANE
    for lo, hi, kind in TILE_KINDS:
        @pl.when((j >= lo) & (j < hi))
        def _(kind=kind):
            if kind == 0:
                z_ref[...] = acc
            elif kind in (1, 2):
                half = ROT_DIM // 2 if kind == 1 else ROT_DIM_IDX // 2
                for c in range(n_chunk):
                    sl = slice(c * LANE, (c + 1) * LANE)
                    z_ref[:, sl] = _rope_chunk(acc[:, sl], tab_ref, kind - 1, half)
            else:
                z_ref[...] = acc
                c = (Z_MISC - Z_RC_WA) // LANE
                sl = slice(c * LANE, (c + 1) * LANE)
                xm = acc[:, sl]
                roped = _rope_chunk(xm, tab_ref, 2, ROT_DIM_IDX // 2)
                lane = lax.broadcasted_iota(jnp.int32, xm.shape, 1)
                is_f = (lane >= M_F) & (lane < M_F + HA)
                z_ref[:, sl] = jnp.where(is_f, _log_sigmoid(xm + bf_ref[...]), roped)


def _inproj(x2d, gain, wz, tabs, bf_row, tm):
    rows, d = x2d.shape
    n_pos_blk = tabs.shape[1] // tm
    grid = (rows // tm, NZ // TN)
    return pl.pallas_call(
        _inproj_kernel,
        grid=grid,
        in_specs=[
            pl.BlockSpec((tm, d), lambda i, j: (i, 0)),
            pl.BlockSpec((1, d), lambda i, j: (0, 0)),
            pl.BlockSpec((TN, d), lambda i, j: (j, 0)),
            pl.BlockSpec((9, tm, LANE), lambda i, j: (0, i % n_pos_blk, 0)),
            pl.BlockSpec((1, LANE), lambda i, j: (0, 0)),
        ],
        out_specs=pl.BlockSpec((tm, TN), lambda i, j: (i, j)),
        out_shape=jax.ShapeDtypeStruct((rows, NZ), F32),
        scratch_shapes=[pltpu.VMEM((tm, d), BF16)],
        compiler_params=_cparams(("parallel", "arbitrary")),
    )(x2d, gain.reshape(1, d), wz, tabs, bf_row)


def _outproj_kernel(oa_ref, ob_ref, oc_ref, x_ref, w_ref, fg_ref, y_ref, *, final):
    acc = jnp.dot(oa_ref[...].astype(BF16), w_ref[0:WA, :], preferred_element_type=F32)
    acc += jnp.dot(ob_ref[...].astype(BF16), w_ref[WA:WA + WB, :], preferred_element_type=F32)
    acc += jnp.dot(oc_ref[...].astype(BF16), w_ref[WA + WB:, :], preferred_element_type=F32)
    y = x_ref[...] + acc
    if final:
        ms = jnp.mean(y * y, axis=-1, keepdims=True)
        y = y * lax.rsqrt(ms + RMS_EPS) * fg_ref[...]
    y_ref[...] = y


def _outproj(oa, ob, oc, x2d, w_bf, fgain, tm, final):
    rows, d = x2d.shape
    row_spec = lambda w: pl.BlockSpec((tm, w), lambda i: (i, 0))
    return pl.pallas_call(
        functools.partial(_outproj_kernel, final=final),
        grid=(rows // tm,),
        in_specs=[row_spec(WA), row_spec(WB), row_spec(WC), row_spec(d),
                  pl.BlockSpec((WA + WB + WC, d), lambda i: (0, 0)),
                  pl.BlockSpec((1, d), lambda i: (0, 0))],
        out_specs=row_spec(d),
        out_shape=jax.ShapeDtypeStruct((rows, d), F32),
        compiler_params=_cparams(("parallel",)),
    )(oa, ob, oc, x2d, w_bf, fgain.reshape(1, d))


def _cumsum_kernel(z_ref, cb_ref, carry_ref):
    @pl.when(pl.program_id(1) == 0)
    def _():
        carry_ref[...] = jnp.zeros_like(carry_ref)

    lf = z_ref[...]
    tm = lf.shape[0]
    r = lax.broadcasted_iota(jnp.int32, (tm, tm), 0)
    c = lax.broadcasted_iota(jnp.int32, (tm, tm), 1)
    tri = jnp.where(r >= c, 1.0, 0.0).astype(F32)
    cum = jnp.dot(tri, lf, precision=HIGHEST, preferred_element_type=F32) + carry_ref[...]
    carry_ref[...] = cum[tm - 1:tm, :]
    for h in range(HA):
        cb_ref[:, h * LANE:(h + 1) * LANE] = jnp.broadcast_to(cum[:, M_F + h:M_F + h + 1], (tm, LANE))


def _fox_cumsum(z, batch, seq, tm=256):
    nblk = seq // tm
    return pl.pallas_call(
        _cumsum_kernel,
        grid=(batch, nblk),
        in_specs=[pl.BlockSpec((tm, LANE), lambda b, i: (b * nblk + i, Z_MISC // LANE))],
        out_specs=pl.BlockSpec((tm, HA * LANE), lambda b, i: (b * nblk + i, 0)),
        out_shape=jax.ShapeDtypeStruct((batch * seq, HA * LANE), F32),
        scratch_shapes=[pltpu.VMEM((1, LANE), F32)],
        compiler_params=_cparams(("parallel", "arbitrary")),
    )(z)


def _silu(g):
    return g / (1.0 + jnp.exp(-g))


def _flash_kernel(*refs, mode, nh, tq):
    if mode == 'fox':
        q_ref, k_ref, v_ref, g_ref, cb_ref, o_ref, qs_ref, m_ref, l_ref, acc_ref = refs
    else:
        q_ref, k_ref, v_ref, g_ref, bias_ref, o_ref, qs_ref, m_ref, l_ref, acc_ref, bt_ref = refs
    qi = pl.program_id(1)
    ki = pl.program_id(2)
    nt = (((1,), (1,)), ((), ()))
    tn = (((0,), (0,)), ((), ()))

    @pl.when(ki == 0)
    def _():
        qs_ref[...] = (q_ref[...] * (DH ** -0.5)).astype(BF16)
        m_ref[...] = jnp.full_like(m_ref, -jnp.inf)
        l_ref[...] = jnp.zeros_like(l_ref)
        acc_ref[...] = jnp.zeros_like(acc_ref)

    def body(diag):
        if mode == 'dsa':
            bt_ref[...] = bias_ref[0].astype(F32)
        elif diag:
            key_i = lax.broadcasted_iota(jnp.int32, (tq, tq), 0)
            qry_i = lax.broadcasted_iota(jnp.int32, (tq, tq), 1)
            causal = key_i <= qry_i
        for h in range(nh):
            hs = slice(h * DH, (h + 1) * DH)
            st = lax.dot_general(k_ref[:, hs].astype(BF16), qs_ref[:, hs], nt, preferred_element_type=F32)
            if mode == 'fox':
                st = st - jnp.concatenate([cb_ref[:, hs]] * (tq // LANE), axis=1)
                if diag:
                    st = jnp.where(causal, st, NEG)
            else:
                st = st + bt_ref[...]
            m_prev = m_ref[h]
            m_new = jnp.maximum(m_prev, jnp.max(st, axis=0, keepdims=True))
            corr = jnp.exp(m_prev - m_new)
            p = jnp.exp(st - m_new)
            l_ref[h] = corr * l_ref[h] + jnp.sum(p, axis=0, keepdims=True)
            pv = lax.dot_general(v_ref[:, hs].astype(BF16), p.astype(BF16), tn, preferred_element_type=F32)
            acc_ref[hs, :] = corr * acc_ref[hs, :] + pv
            m_ref[h] = m_new

    @pl.when(ki < qi)
    def _():
        body(False)

    @pl.when(ki == qi)
    def _():
        body(True)
        for h in range(nh):
            hs = slice(h * DH, (h + 1) * DH)
            o_ref[:, hs] = (acc_ref[hs, :] / l_ref[h]).T * _silu(g_ref[:, hs])


def _flash(z, extra, batch, seq, mode, tq=512):
    nh = HA if mode == 'fox' else HB
    wgrp = nh * DH
    nq = seq // tq
    cq, ck, cv, cg = ((Z_FA_Q, Z_FA_K, Z_FA_V, Z_FA_G) if mode == 'fox' else (Z_SB_Q, Z_SB_K, Z_SB_V, Z_SB_G))
    qmap = lambda col: (lambda b, i, k: (b * nq + i, col // wgrp))
    kmap = lambda col: (lambda b, i, k: (b * nq + jnp.minimum(k, i), col // wgrp))
    in_specs = [pl.BlockSpec((tq, wgrp), qmap(cq)), pl.BlockSpec((tq, wgrp), kmap(ck)),
                pl.BlockSpec((tq, wgrp), kmap(cv)), pl.BlockSpec((tq, wgrp), qmap(cg))]
    scratch = [pltpu.VMEM((tq, wgrp), BF16), pltpu.VMEM((nh, 1, tq), F32), pltpu.VMEM((nh, 1, tq), F32),
               pltpu.VMEM((wgrp, tq), F32)]
    if mode == 'fox':
        in_specs += [pl.BlockSpec((tq, wgrp), lambda b, i, k: (b * nq + jnp.minimum(k, i), 0))]
    else:
        in_specs += [pl.BlockSpec((1, tq, tq), lambda b, i, k: (b, jnp.minimum(k, i), i))]
        scratch += [pltpu.VMEM((tq, tq), F32)]
    args = (z, z, z, z, extra)
    return pl.pallas_call(
        functools.partial(_flash_kernel, mode=mode, nh=nh, tq=tq),
        grid=(batch, nq, nq),
        in_specs=in_specs,
        out_specs=pl.BlockSpec((tq, wgrp), lambda b, i, k: (b * nq + i, 0)),
        out_shape=jax.ShapeDtypeStruct((batch * seq, wgrp), F32),
        scratch_shapes=scratch,
        compiler_params=_cparams(("parallel", "parallel", "arbitrary")),
    )(*args)


def _sort_key(x):
    b = pltpu.bitcast(x + 0.0, jnp.int32)
    return b ^ ((b >> 31) & 0x7FFFFFFF)


def _kth_threshold(count_ge, shape, k):
    def step(it, u):
        bit = jnp.left_shift(jnp.int32(1), 31 - it)
        cand = u | bit
        cnt = count_ge(cand ^ INT_MIN)
        return jnp.where(cnt >= k, cand, u)

    u = lax.fori_loop(0, 32, step, jnp.zeros(shape, jnp.int32))
    return u ^ INT_MIN


def _dsa_select_kernel(qi_ref, qm_ref, km_ref, bias_ref, key_ref, qst_ref, *, tq, n_sel):
    i = pl.program_id(1)
    seq = km_ref.shape[0]
    n_kc = seq // tq
    nt = (((1,), (1,)), ((), ()))
    for h in range(HI):
        qst_ref[h] = qi_ref[...][:, h * DI:(h + 1) * DI].astype(BF16)
    wt = (qm_ref[...] * ((DI ** -0.5) * (HI ** -0.5))).T

    def score_chunk(c, _):
        c0 = pl.multiple_of(c * tq, tq)
        kc = km_ref[pl.ds(c0, tq), :][:, M_IXK:M_IXK + DI].astype(BF16)
        sc = jnp.zeros((tq, tq), F32)
        for h in range(HI):
            d = lax.dot_general(kc, qst_ref[h], nt, preferred_element_type=F32)
            sc = sc + jnp.maximum(d, 0.0) * wt[M_IXW + h:M_IXW + h + 1, :]
        key = _sort_key(sc)
        kr = lax.broadcasted_iota(jnp.int32, (tq, tq), 0)
        qc = lax.broadcasted_iota(jnp.int32, (tq, tq), 1)
        key_ref[pl.ds(c0, tq), :] = jnp.where((c < i) | (kr <= qc), key, INT_MIN)
        return 0

    lax.fori_loop(0, i + 1, score_chunk, 0)

    def count_ge(t):
        def chunk(c, acc):
            c0 = pl.multiple_of(c * tq, tq)
            hit = jnp.where(key_ref[pl.ds(c0, tq), :] >= t, 1.0, 0.0)
            return acc + jnp.sum(hit.reshape(tq // 8, 8, tq), axis=0)

        acc = lax.fori_loop(0, i + 1, chunk, jnp.zeros((8, tq), F32))
        return jnp.sum(acc, axis=0, keepdims=True)

    thr = _kth_threshold(count_ge, (1, tq), float(n_sel))

    def write_chunk(c, _):
        c0 = pl.multiple_of(c * tq, tq)
        kk = key_ref[pl.ds(c0, tq), :]
        sel = (kk >= thr) & (kk != INT_MIN)
        bias_ref[0, pl.ds(c0, tq), :] = jnp.where(sel, 0.0, NEG).astype(BF16)
        return 0

    def fill_chunk(c, _):
        c0 = pl.multiple_of(c * tq, tq)
        bias_ref[0, pl.ds(c0, tq), :] = jnp.full((tq, tq), NEG, BF16)
        return 0

    lax.fori_loop(0, i + 1, write_chunk, 0)
    lax.fori_loop(i + 1, n_kc, fill_chunk, 0)


def _dsa_select(z, batch, seq, tq=256):
    nq = seq // tq
    n_sel = min(TOPK, seq // 4)
    return pl.pallas_call(
        functools.partial(_dsa_select_kernel, tq=tq, n_sel=n_sel),
        grid=(batch, nq),
        in_specs=[pl.BlockSpec((tq, HI * DI), lambda b, i: (b * nq + i, Z_IX_Q // (HI * DI))),
                  pl.BlockSpec((tq, LANE), lambda b, i: (b * nq + i, Z_MISC // LANE)),
                  pl.BlockSpec((seq, LANE), lambda b, i: (b, Z_MISC // LANE))],
        out_specs=pl.BlockSpec((1, seq, tq), lambda b, i: (b, 0, i)),
        out_shape=jax.ShapeDtypeStruct((batch, seq, seq), BF16),
        scratch_shapes=[pltpu.VMEM((seq, tq), jnp.int32), pltpu.VMEM((HI, tq, DI), BF16)],
        compiler_params=_cparams(("parallel", "parallel")),
    )(z, z, z)


DIMS_NN = (((1,), (0,)), ((), ()))
DIMS_NT = (((1,), (1,)), ((), ()))
DIMS_TN = (((0,), (0,)), ((), ()))


def _dot_hi(a, b):
    return jnp.dot(a, b, precision=HIGHEST, preferred_element_type=F32)


def _split(x):
    hi = x.astype(BF16)
    lo = (x - hi.astype(F32)).astype(BF16)
    return hi, lo


def _dot3(a, b, dims=DIMS_NN):
    ah, al = a if isinstance(a, tuple) else _split(a)
    bh, bl = b if isinstance(b, tuple) else _split(b)
    d = lambda x, y: lax.dot_general(x, y, dims, preferred_element_type=F32)
    return d(ah, bh) + (d(ah, bl) + d(al, bh))


def _dot2(x, m_bf):
    hi, lo = _split(x)
    return (jnp.dot(hi, m_bf, preferred_element_type=F32) + jnp.dot(lo, m_bf, preferred_element_type=F32))


def _dot2l(m_bf, x):
    hi, lo = _split(x)
    return (jnp.dot(m_bf, hi, preferred_element_type=F32) + jnp.dot(m_bf, lo, preferred_element_type=F32))


def _rwkv_premix(cur, prev, p_ref, wb_ref, ab_ref, hones_ref):
    mix = lambda c, pv, mu: c + (pv - c) * mu
    xr = mix(cur['r'], prev['r'], p_ref[0:1, :])
    xk = mix(cur['k'], prev['k'], p_ref[1:2, :])
    xv = mix(cur['v'], prev['v'], p_ref[2:3, :])
    xwa = mix(cur['wa'], prev['wa'], p_ref[3:4, 0:LANE])
    hones = hones_ref[...]
    w_raw = p_ref[4:5, :] + _dot3(jnp.tanh(xwa), (wb_ref[0], wb_ref[1]))
    nw = -w_raw
    softplus = jnp.maximum(nw, 0.0) + jnp.log(1.0 + jnp.exp(-jnp.abs(nw)))
    lw = -jnp.exp(-softplus - 0.5)
    a = 1.0 / (1.0 + jnp.exp(-(p_ref[5:6, :] + _dot3(xwa, (ab_ref[0], ab_ref[1])))))
    kk0 = xk * p_ref[6:7, :]
    ss = _dot2(kk0 * kk0, hones)
    kk = kk0 / jnp.maximum(jnp.sqrt(ss), 1e-12)
    k2 = xk * (1.0 + (a - 1.0) * p_ref[7:8, :])
    bonus = _dot2(xr * k2 * p_ref[8:9, :], hones) * xv
    return xr, k2, xv, lw, kk, a, bonus


def _rwkv_finish(out, bonus, g, p_ref, hones_ref):
    hones = hones_ref[...]
    mean = _dot2(out, hones) * (1.0 / DC)
    d = out - mean
    var = _dot2(d * d, hones) * (1.0 / DC)
    y = d * lax.rsqrt(var + GN_EPS) * p_ref[9:10, :] + p_ref[10:11, :] + bonus
    return y * _silu(g)


def _rwkv_params(mu, w0, w_b, a0, a_b, k_k, k_a, r_k, ln_w, ln_b):
    pad = lambda v: jnp.concatenate([v, jnp.zeros((WC - v.shape[0],), F32)])
    rows = [mu[:WC], mu[WC:2 * WC], mu[2 * WC:3 * WC], pad(mu[3 * WC:]), w0, a0, k_k, k_a,
            r_k.reshape(WC), ln_w, ln_b]
    rows += [jnp.zeros((WC,), F32)] * (16 - len(rows))
    p = jnp.stack(rows, axis=0)
    split2 = lambda m: jnp.stack(_split(m), axis=0)
    wb = split2(jnp.zeros((LANE, WC), F32).at[0:LORA].set(w_b))
    ab = split2(jnp.zeros((LANE, WC), F32).at[LORA:2 * LORA].set(a_b))
    hid = jnp.arange(WC) // DC
    hones = (hid[:, None] == hid[None, :]).astype(BF16)
    return p, wb, ab, hones


def _rwkv_chunk_terms(units, first, strict, incl, eye):
    C = units[0][0].shape[0]
    cat0 = lambda xs: jnp.concatenate(xs, axis=0)
    cat1 = lambda xs: jnp.concatenate(xs, axis=1)
    stack = lambda x: cat0([jnp.where(first, x, 0.0), jnp.where(first, 0.0, x)])
    dup = lambda x: cat0([x, x])
    la = [stack(u[0]) for u in units]
    lr = [stack(u[1]) for u in units]
    vs = [stack(u[6]) for u in units]
    xx = [_dot3(cat0([a, r]), cat0([dup(u[2]), dup(u[3])]), DIMS_NT) for a, r, u in zip(la, lr, units)]
    n_m = [jnp.where(strict, x[:2 * C, :2 * C], 0.0) for x in xx]
    m_m = [jnp.where(strict, x[:2 * C, 2 * C:], 0.0) for x in xx]
    pb = [jnp.where(incl, x[2 * C:, :2 * C], 0.0) for x in xx]
    pk = [jnp.where(incl, x[2 * C:, 2 * C:], 0.0) for x in xx]
    t_inv = [jnp.where(eye, 1.0, 0.0) + n for n in n_m]
    pw_s = [_split(n) for n in n_m]
    sz = 2
    while sz < C:
        pw_s = [_split(_dot3(s, s)) for s in pw_s]
        t_inv = [t + _dot3(s, t) for s, t in zip(pw_s, t_inv)]
        sz *= 2
    mv = [_dot3(m, v) for m, v in zip(m_m, vs)]
    ta = [_dot3(t, cat1([a, m])) for t, a, m in zip(t_inv, la, mv)]
    big = [_split(cat0([t, cat1([jnp.zeros_like(v), v])])) for t, v in zip(ta, vs)]
    pp = [_dot3(cat1([b, k]), g) for b, k, g in zip(pb, pk, big)]
    php = [_dot3(cat0([stack(u[4]), stack(u[5])]), g, DIMS_TN) for u, g in zip(units, big)]
    terms = []
    for u, r, p, h in zip(units, lr, pp, php):
        r1s = r + p[:, :LANE]
        ovs = p[:, LANE:]
        phi = jnp.where(eye, jnp.broadcast_to(u[7], (LANE, LANE)), 0.0) + h[:, :LANE]
        terms.append((cat0([phi, r1s[:C] + r1s[C:]]), ovs[:C] + ovs[C:], h[:, LANE:]))
    return terms


def _rwkv_chunk_kernel(r_ref, k_ref, v_ref, wa_ref, g_ref, p_ref, wb_ref, ab_ref, hones_ref,
                       o_ref, st_ref, s_ref, pr_ref, pk_ref, pv_ref, pwa_ref):
    step = pl.program_id(1)
    R = r_ref.shape[0]
    C = RC_CHUNK
    ncs = R // C

    @pl.when(step == 0)
    def _():
        s_ref[...] = jnp.zeros_like(s_ref)
        pr_ref[...] = jnp.zeros_like(pr_ref)
        pk_ref[...] = jnp.zeros_like(pk_ref)
        pv_ref[...] = jnp.zeros_like(pv_ref)
        pwa_ref[...] = jnp.zeros_like(pwa_ref)

    def shifted(x, carry_ref):
        row = lax.broadcasted_iota(jnp.int32, x.shape, 0)
        pv = jnp.where(row == 0, carry_ref[...], pltpu.roll(x, 1, 0))
        carry_ref[...] = x[R - 1:R, :]
        return pv

    cur = {'r': r_ref[...], 'k': k_ref[...], 'v': v_ref[...], 'wa': wa_ref[...]}
    prev = {'r': shifted(cur['r'], pr_ref), 'k': shifted(cur['k'], pk_ref),
            'v': shifted(cur['v'], pv_ref), 'wa': shifted(cur['wa'], pwa_ref)}
    xr, k2, xv, lw, kk, a, bonus = _rwkv_premix(cur, prev, p_ref, wb_ref, ab_ref, hones_ref)

    ri = lax.broadcasted_iota(jnp.int32, (R, R), 0)
    ci = lax.broadcasted_iota(jnp.int32, (R, R), 1)
    tril = jnp.where((ri >= ci) & ((ri // C) == (ci // C)), 1.0, 0.0).astype(BF16)
    lc = _dot2l(tril, lw)
    lend = jnp.concatenate([jnp.broadcast_to(lc[(c + 1) * C - 1:(c + 1) * C, :], (C, WC)) for c in range(ncs)],
                           axis=0)
    e_neg = jnp.exp(-lc)
    e_end = jnp.exp(lend - lc)
    bn = kk * a
    a_t = -kk * jnp.exp(lc - lw)
    r_t = xr * jnp.exp(lc)
    b_t = bn * e_neg
    k_t = k2 * e_neg
    b_h = bn * e_end
    k_h = k2 * e_end

    first = lax.broadcasted_iota(jnp.int32, (1, LANE), 1) < DC
    pr_i = lax.broadcasted_iota(jnp.int32, (LANE, LANE), 0)
    pc_i = lax.broadcasted_iota(jnp.int32, (LANE, LANE), 1)
    same = (pr_i // DC) == (pc_i // DC)
    strict = same & (pr_i > pc_i)
    incl = same & (pr_i >= pc_i)
    eye = pr_i == pc_i

    units = []
    for c in range(ncs):
        rs = slice(c * C, (c + 1) * C)
        g_row = jnp.exp(lc[(c + 1) * C - 1:(c + 1) * C, :])
        for p in range(HC // 2):
            ps = slice(p * LANE, (p + 1) * LANE)
            units.append((a_t[rs, ps], r_t[rs, ps], b_t[rs, ps], k_t[rs, ps], b_h[rs, ps], k_h[rs, ps],
                          xv[rs, ps], g_row[:, ps]))
    terms = _rwkv_chunk_terms(units, first, strict, incl, eye)

    states = [s_ref[p] for p in range(HC // 2)]
    out_rows = []
    for c in range(ncs):
        outs = []
        for p in range(HC // 2):
            lhs, ov, psi = terms[c * (HC // 2) + p]
            sr = _dot3(lhs, states[p])
            outs.append(sr[LANE:] + ov)
            states[p] = sr[:LANE] + psi
        out_rows.append(jnp.concatenate(outs, axis=1))
    out = jnp.concatenate(out_rows, axis=0)
    o_ref[...] = _rwkv_finish(out, bonus, g_ref[...], p_ref, hones_ref)
    for p in range(HC // 2):
        s_ref[p] = states[p]

    @pl.when(step == pl.num_programs(1) - 1)
    def _():
        for p in range(HC // 2):
            st_ref[0, p] = states[p]


def _rwkv_prompt(z, params, batch, seq):
    p, wb, ab, hones = params
    R = min(RC_ROWS, seq)
    nc = seq // R
    col = lambda off, w: (lambda b, i: (b * nc + i, off // w))
    const = lambda shp: pl.BlockSpec(shp, lambda b, i: (0,) * len(shp))
    o, st = pl.pallas_call(
        _rwkv_chunk_kernel,
        grid=(batch, nc),
        in_specs=[pl.BlockSpec((R, WC), col(Z_RC_R, WC)), pl.BlockSpec((R, WC), col(Z_RC_K, WC)),
                  pl.BlockSpec((R, WC), col(Z_RC_V, WC)), pl.BlockSpec((R, LANE), col(Z_RC_WA, LANE)),
                  pl.BlockSpec((R, WC), col(Z_RC_G, WC)),
                  const((16, WC)), const((2, LANE, WC)), const((2, LANE, WC)), const((WC, WC))],
        out_specs=[pl.BlockSpec((R, WC), lambda b, i: (b * nc + i, 0)),
                   pl.BlockSpec((1, HC // 2, LANE, LANE), lambda b, i: (b, 0, 0, 0))],
        out_shape=[jax.ShapeDtypeStruct((batch * seq, WC), F32),
                   jax.ShapeDtypeStruct((batch, HC // 2, LANE, LANE), F32)],
        scratch_shapes=[pltpu.VMEM((HC // 2, LANE, LANE), F32), pltpu.VMEM((1, WC), F32),
                        pltpu.VMEM((1, WC), F32), pltpu.VMEM((1, WC), F32), pltpu.VMEM((1, LANE), F32)],
        compiler_params=_cparams(("parallel", "arbitrary")),
    )(z, z, z, z, z, p, wb, ab, hones)
    st = st.reshape(batch, HC // 2, 2, DC, 2, DC)
    st = jnp.stack([st[:, :, 0, :, 0, :], st[:, :, 1, :, 1, :]], axis=2).reshape(batch, HC, DC, DC)
    return o, jnp.swapaxes(st, -1, -2)


def _paged_attn_kernel(*refs, mode, pps, nh):
    pt_ref = refs[0]
    q_ref, kn_ref, vn_ref, g_ref = refs[1:5]
    k_refs = refs[5:5 + pps]
    v_refs = refs[5 + pps:5 + 2 * pps]
    rest = refs[5 + 2 * pps:]
    if mode == 'fox':
        lf_refs = rest[:pps]
        lfn_ref = rest[pps]
        o_ref, qm_ref, m_ref, l_ref, acc_ref, carry_ref = rest[pps + 1:]
    else:
        bias_ref, biasn_ref = rest[:2]
        o_ref, qm_ref, m_ref, l_ref, acc_ref = rest[2:]
    del pt_ref
    step = pl.program_id(1)
    w = nh * DH
    hrow = lax.broadcasted_iota(jnp.int32, (8, w), 0)
    hcol = lax.broadcasted_iota(jnp.int32, (8, w), 1) // DH
    own = hrow == hcol

    @pl.when(step == 0)
    def _():
        qm_ref[...] = jnp.where(own, q_ref[0] * (DH ** -0.5), 0.0).astype(BF16)
        m_ref[...] = jnp.full_like(m_ref, -jnp.inf)
        l_ref[...] = jnp.zeros_like(l_ref)
        acc_ref[...] = jnp.zeros_like(acc_ref)
        if mode == 'fox':
            carry_ref[...] = jnp.zeros_like(carry_ref)

    nt = (((1,), (1,)), ((), ()))
    qm = qm_ref[...]
    s_parts = []
    if mode == 'fox':
        ur = lax.broadcasted_iota(jnp.int32, (LANE, LANE), 0)
        uc = lax.broadcasted_iota(jnp.int32, (LANE, LANE), 1)
        upper = jnp.where(ur <= uc, 1.0, 0.0).astype(F32)
        carry = carry_ref[...]
    cat = lambda ref: jnp.concatenate([ref[h] for h in range(nh)], axis=1).astype(BF16)
    for i in range(pps):
        s = lax.dot_general(qm, cat(k_refs[i]), nt, preferred_element_type=F32)
        if mode == 'fox':
            cum = _dot_hi(lf_refs[i][...], upper) + carry
            carry = jnp.broadcast_to(cum[:, LANE - 1:LANE], (8, LANE))
            s = s - cum
        else:
            s = s + bias_ref[0, :, i * LANE:(i + 1) * LANE]
        s_parts.append(s)
    if mode == 'fox':
        carry_ref[...] = carry
    s_all = jnp.concatenate(s_parts, axis=1)
    m_prev = m_ref[...]
    m_new = jnp.maximum(m_prev, jnp.max(s_all, axis=1, keepdims=True))
    corr = jnp.exp(m_prev - m_new)
    p_all = jnp.exp(s_all - m_new)
    l_ref[...] = corr * l_ref[...] + jnp.sum(p_all, axis=1, keepdims=True)
    pv = jnp.zeros((8, w), F32)
    for i in range(pps):
        pv = pv + jnp.dot(p_all[:, i * LANE:(i + 1) * LANE].astype(BF16), cat(v_refs[i]),
                          preferred_element_type=F32)
    acc_ref[...] = corr * acc_ref[...] + pv
    m_ref[...] = m_new

    @pl.when(step == pl.num_programs(1) - 1)
    def _():
        kn = kn_ref[0].astype(BF16).astype(F32)
        sn = jnp.sum(qm_ref[...].astype(F32) * kn, axis=1, keepdims=True)
        if mode == 'fox':
            sn = sn - (carry_ref[:, 0:1] + lfn_ref[0])
        else:
            sn = sn + biasn_ref[0, :, 0:1]
        m_p = m_ref[...]
        m_n = jnp.maximum(m_p, sn)
        cr = jnp.exp(m_p - m_n)
        pn = jnp.exp(sn - m_n)
        lsum = cr * l_ref[...] + pn
        acc = cr * acc_ref[...] + pn * vn_ref[0]
        o = jnp.sum(jnp.where(own, acc / lsum, 0.0), axis=0, keepdims=True)
        o_ref[0] = o * _silu(g_ref[0])


def _paged_attn(zs3, cache_k, cache_v, layer, page_table, extra, mode):
    db = zs3.shape[0]
    n_pages = page_table.shape[1]
    pps = min(PPS_ATTN, n_pages)
    nsteps = n_pages // pps
    nh = HA if mode == 'fox' else HB
    w = nh * DH
    cq, ck, cv, cg = ((Z_FA_Q, Z_FA_K, Z_FA_V, Z_FA_G) if mode == 'fox' else (Z_SB_Q, Z_SB_K, Z_SB_V, Z_SB_G))
    zspec = lambda col: pl.BlockSpec((1, 1, w), lambda b, s, pt: (b, 0, col // w))
    page_map = lambda i: (lambda b, s, pt: (layer, pt[b * n_pages + s * pps + i], 0, 0))
    page_map5 = lambda i: (lambda b, s, pt: (layer, pt[b * n_pages + s * pps + i], 0, 0, 0))
    in_specs = [zspec(cq), zspec(ck), zspec(cv), zspec(cg)]
    in_specs += [pl.BlockSpec((None, None, nh, PAGE, DH), page_map5(i)) for i in range(pps)]
    in_specs += [pl.BlockSpec((None, None, nh, PAGE, DH), page_map5(i)) for i in range(pps)]
    args = [zs3, zs3, zs3, zs3] + [cache_k] * pps + [cache_v] * pps
    scratch = [pltpu.VMEM((8, w), BF16), pltpu.VMEM((8, 1), F32), pltpu.VMEM((8, 1), F32),
               pltpu.VMEM((8, w), F32)]
    if mode == 'fox':
        logf_t, lf_new = extra
        in_specs += [pl.BlockSpec((None, None, 8, PAGE), page_map(i)) for i in range(pps)]
        in_specs += [pl.BlockSpec((1, 8, 1), lambda b, s, pt: (b, 0, 0))]
        args += [logf_t] * pps + [lf_new]
        scratch += [pltpu.VMEM((8, LANE), F32)]
    else:
        bias = extra
        in_specs += [pl.BlockSpec((1, 1, pps * LANE), lambda b, s, pt: (b, 0, s)),
                     pl.BlockSpec((1, 1, LANE), lambda b, s, pt: (b, 0, n_pages))]
        args += [bias, bias]
    grid_spec = pltpu.PrefetchScalarGridSpec(
        num_scalar_prefetch=1, grid=(db, nsteps), in_specs=in_specs,
        out_specs=pl.BlockSpec((1, 1, w), lambda b, s, pt: (b, 0, 0)),
        scratch_shapes=scratch)
    return pl.pallas_call(
        functools.partial(_paged_attn_kernel, mode=mode, pps=pps, nh=nh),
        grid_spec=grid_spec,
        out_shape=jax.ShapeDtypeStruct((db, 1, w), F32),
        compiler_params=_cparams(("parallel", "arbitrary")),
    )(page_table.reshape(-1), *args)


def _sample_select_kernel(*refs, pps, n_sel):
    pt_ref = refs[0]
    qi_ref, wi_ref, kin_ref = refs[1:4]
    ki_refs = refs[4:4 + pps]
    bias_ref, key_ref = refs[4 + pps:]
    del pt_ref
    step = pl.program_id(1)
    nsteps = pl.num_programs(1)
    past = key_ref.shape[1] - LANE
    qb = qi_ref[0].astype(BF16)
    w = wi_ref[0] * ((DI ** -0.5) * (HI ** -0.5))

    def score(kmat_t):
        d = jnp.dot(qb, kmat_t.astype(BF16), preferred_element_type=F32)
        return jnp.sum(jnp.maximum(d, 0.0) * w, axis=0, keepdims=True)

    for i in range(pps):
        off = pl.multiple_of((step * pps + i) * LANE, LANE)
        key_ref[:, pl.ds(off, LANE)] = _sort_key(score(ki_refs[i][...]))

    @pl.when(step == nsteps - 1)
    def _():
        dn = jnp.sum(qb.astype(F32) * kin_ref[0].astype(BF16).astype(F32), axis=1, keepdims=True)
        sn = jnp.sum(jnp.maximum(dn, 0.0) * w, axis=0, keepdims=True)
        lane = lax.broadcasted_iota(jnp.int32, (1, LANE), 1)
        key_ref[:, past:past + LANE] = jnp.where(lane == 0, _sort_key(jnp.broadcast_to(sn, (1, LANE))), INT_MIN)
        keys = key_ref[...]

        def count_ge(t):
            return jnp.sum(jnp.where(keys >= t, 1.0, 0.0), axis=1, keepdims=True)

        thr = _kth_threshold(count_ge, (1, 1), float(n_sel))
        sel = (keys >= thr) & (keys != INT_MIN)
        bias_ref[0] = jnp.where(sel, 0.0, NEG)


def _sample_select(qi3, wi3, kin3, cache_ki, layer, page_table):
    db = qi3.shape[0]
    n_pages = page_table.shape[1]
    pps = min(PPS_IDX, n_pages)
    past = n_pages * PAGE
    n_sel = min(TOPK, (past + 1) // 4)
    page_map = lambda i: (lambda b, s, pt: (layer, pt[b * n_pages + s * pps + i], 0, 0))
    in_specs = [pl.BlockSpec((1, HI, DI), lambda b, s, pt: (b, 0, 0)),
                pl.BlockSpec((1, HI, 1), lambda b, s, pt: (b, 0, 0)),
                pl.BlockSpec((1, 1, DI), lambda b, s, pt: (b, 0, 0))]
    in_specs += [pl.BlockSpec((None, None, DI, PAGE), page_map(i)) for i in range(pps)]
    grid_spec = pltpu.PrefetchScalarGridSpec(
        num_scalar_prefetch=1, grid=(db, n_pages // pps), in_specs=in_specs,
        out_specs=pl.BlockSpec((1, 1, past + LANE), lambda b, s, pt: (b, 0, 0)),
        scratch_shapes=[pltpu.VMEM((1, past + LANE), jnp.int32)])
    return pl.pallas_call(
        functools.partial(_sample_select_kernel, pps=pps, n_sel=n_sel),
        grid_spec=grid_spec,
        out_shape=jax.ShapeDtypeStruct((db, 1, past + LANE), F32),
        compiler_params=_cparams(("parallel", "arbitrary")),
    )(page_table.reshape(-1), qi3, wi3, kin3, *([cache_ki] * pps))


def _rwkv_pre_kernel(r_ref, k_ref, v_ref, wa_ref, sr_ref, sk_ref, sv_ref, swa_ref, p_ref, wb_ref, ab_ref,
                     hones_ref, or_ref, ok_ref, ov_ref, ow_ref, onk_ref, oka_ref, ob_ref):
    cur = {'r': r_ref[...], 'k': k_ref[...], 'v': v_ref[...], 'wa': wa_ref[...]}
    prev = {'r': sr_ref[...], 'k': sk_ref[...], 'v': sv_ref[...], 'wa': swa_ref[...]}
    xr, k2, xv, lw, kk, a, bonus = _rwkv_premix(cur, prev, p_ref, wb_ref, ab_ref, hones_ref)
    or_ref[...] = xr
    ok_ref[...] = k2
    ov_ref[...] = xv
    ow_ref[...] = jnp.exp(lw)
    onk_ref[...] = -kk
    oka_ref[...] = kk * a
    ob_ref[...] = bonus


def _rwkv_step_kernel(s_ref, r_ref, k_ref, v_ref, w_ref, nk_ref, ka_ref, so_ref, o_ref):
    s = s_ref[...]
    sa = jnp.sum(s * nk_ref[...], axis=-1, keepdims=True)
    s2 = s * w_ref[...] + sa * ka_ref[...] + v_ref[...] * k_ref[...]
    so_ref[...] = s2
    o_ref[...] = jnp.sum(s2 * r_ref[...], axis=-1, keepdims=True)


def _rwkv_post_kernel(o_ref, b_ref, g_ref, p_ref, hones_ref, y_ref):
    y_ref[...] = _rwkv_finish(o_ref[...], b_ref[...], g_ref[...], p_ref, hones_ref)


def _rwkv_sample(zs, shift_prev, wkv, params, bb=8):
    p, wb, ab, hones = params
    db = zs.shape[0]
    full = lambda a: pl.BlockSpec(a.shape, lambda i: (0,) * a.ndim)
    zcol = lambda off, w: pl.BlockSpec((db, w), lambda i: (0, off // w))
    swa = jnp.concatenate([shift_prev[:, 3 * WC:], jnp.zeros((db, LANE - 2 * LORA), F32)], axis=1)
    sr, sk, sv = shift_prev[:, :WC], shift_prev[:, WC:2 * WC], shift_prev[:, 2 * WC:3 * WC]
    rows = pl.pallas_call(
        _rwkv_pre_kernel,
        grid=(1,),
        in_specs=[zcol(Z_RC_R, WC), zcol(Z_RC_K, WC), zcol(Z_RC_V, WC), zcol(Z_RC_WA, LANE),
                  full(sr), full(sk), full(sv), full(swa), full(p), full(wb), full(ab), full(hones)],
        out_specs=[pl.BlockSpec((db, WC), lambda i: (0, 0))] * 7,
        out_shape=[jax.ShapeDtypeStruct((db, WC), F32)] * 7,
        compiler_params=_cparams(("arbitrary",)),
    )(zs, zs, zs, zs, sr, sk, sv, swa, p, wb, ab, hones)
    xr, k2, xv, dec, nkk, kka, bonus = rows
    as_row = lambda t: t.reshape(db, HC, 1, DC)
    rspec = pl.BlockSpec((bb, HC, 1, DC), lambda i: (i, 0, 0, 0))
    cspec = pl.BlockSpec((bb, HC, DC, 1), lambda i: (i, 0, 0, 0))
    sspec = pl.BlockSpec((bb, HC, DC, DC), lambda i: (i, 0, 0, 0))
    s_new, o_col = pl.pallas_call(
        _rwkv_step_kernel,
        grid=(db // bb,),
        in_specs=[sspec, rspec, rspec, cspec, rspec, rspec, rspec],
        out_specs=[sspec, cspec],
        out_shape=[jax.ShapeDtypeStruct((db, HC, DC, DC), F32), jax.ShapeDtypeStruct((db, HC, DC, 1), F32)],
        compiler_params=_cparams(("parallel",)),
    )(wkv, as_row(xr), as_row(k2), xv.reshape(db, HC, DC, 1), as_row(dec), as_row(nkk), as_row(kka))
    o_rows = o_col.reshape(db, WC)
    y = pl.pallas_call(
        _rwkv_post_kernel,
        grid=(1,),
        in_specs=[full(o_rows), full(bonus), zcol(Z_RC_G, WC), full(p), full(hones)],
        out_specs=pl.BlockSpec((db, WC), lambda i: (0, 0)),
        out_shape=jax.ShapeDtypeStruct((db, WC), F32),
        compiler_params=_cparams(("arbitrary",)),
    )(o_rows, bonus, zs, p, hones)
    return y, s_new


def _shift_cols(z_last):
    return jnp.concatenate([z_last[..., Z_RC_R:Z_RC_R + 3 * WC], z_last[..., Z_RC_WA:Z_RC_WA + 2 * LORA]], axis=-1)


def _layer_prompt(x2d, batch, seq, gain, wz, bf_row, tabs, rw_params, wo_bf, fgain, final):
    z = _inproj(x2d, gain, wz, tabs, bf_row, tm=min(INPROJ_TM, seq))
    c_t = _fox_cumsum(z, batch, seq, tm=min(CUM_TM, seq))
    o_a = _flash(z, c_t, batch, seq, 'fox', tq=min(FLASH_TQ, seq))
    bias = _dsa_select(z, batch, seq, tq=min(SEL_TQ, seq))
    o_b = _flash(z, bias, batch, seq, 'dsa', tq=min(FLASH_TQ, seq))
    o_c, wkv = _rwkv_prompt(z, rw_params, batch, seq)
    y = _outproj(o_a, o_b, o_c, x2d, wo_bf, fgain, tm=min(OUT_TM, seq), final=final)
    z3 = z.reshape(batch, seq, NZ)
    st = (z3[..., Z_FA_K:Z_FA_K + WA].reshape(batch, seq, HA, DH),
          z3[..., Z_FA_V:Z_FA_V + WA].reshape(batch, seq, HA, DH),
          z3[..., Z_MISC + M_F:Z_MISC + M_F + HA],
          z3[..., Z_SB_K:Z_SB_K + WB].reshape(batch, seq, HB, DH),
          z3[..., Z_SB_V:Z_SB_V + WB].reshape(batch, seq, HB, DH),
          z3[..., Z_MISC + M_IXK:Z_MISC + M_IXK + DI],
          wkv, _shift_cols(z3[:, -1]))
    return y, st


def _layer_sample(xs2d, layer, caches, page_table, gain, wz, bf_row, tabs, rw_params, wo_bf, fgain, final):
    c_fk, c_fv, c_flt, c_dk, c_dv, c_di, s_wkv, s_shift = caches
    db = xs2d.shape[0]
    zs = _inproj(xs2d, gain, wz, tabs, bf_row, tm=db)
    zs3 = zs.reshape(db, 1, NZ)
    misc = zs[:, Z_MISC:Z_MISC + LANE]
    lf_new = jnp.concatenate([misc[:, M_F:M_F + HA], jnp.zeros((db, 8 - HA), F32)], axis=1).reshape(db, 8, 1)
    o_a = _paged_attn(zs3, c_fk, c_fv, layer, page_table, (c_flt, lf_new), 'fox')
    qi3 = zs[:, Z_IX_Q:Z_IX_Q + HI * DI].reshape(db, HI, DI)
    wi3 = misc[:, M_IXW:M_IXW + HI].reshape(db, HI, 1)
    kin3 = misc[:, M_IXK:M_IXK + DI].reshape(db, 1, DI)
    bias = _sample_select(qi3, wi3, kin3, c_di, layer, page_table)
    o_b = _paged_attn(zs3, c_dk, c_dv, layer, page_table, bias, 'dsa')
    o_c, wkv = _rwkv_sample(zs, s_shift[layer], s_wkv[layer], rw_params)
    y = _outproj(o_a.reshape(db, WA), o_b.reshape(db, WB), o_c, xs2d, wo_bf, fgain, tm=db, final=final)
    st = (zs[:, Z_FA_K:Z_FA_K + WA].reshape(db, 1, HA, DH),
          zs[:, Z_FA_V:Z_FA_V + WA].reshape(db, 1, HA, DH),
          misc[:, M_F:M_F + HA].reshape(db, 1, HA),
          zs[:, Z_SB_K:Z_SB_K + WB].reshape(db, 1, HB, DH),
          zs[:, Z_SB_V:Z_SB_V + WB].reshape(db, 1, HB, DH),
          misc[:, M_IXK:M_IXK + DI].reshape(db, 1, DI),
          wkv, _shift_cols(zs))
    return y, st


def kernel(x_prompt, x_sample, cache_fox_k, cache_fox_v, cache_fox_logf, cache_dsa_k, cache_dsa_v, cache_dsa_kidx, state_rwkv_wkv, state_rwkv_shift, page_table, norm_gain, w_in, fox_forget_bias, rwkv_mu, rwkv_w0, rwkv_w_lora_b, rwkv_a0, rwkv_a_lora_b, rwkv_k_k, rwkv_k_a, rwkv_r_k, rwkv_ln_w, rwkv_ln_b, w_out, final_gain):
    batch, seq, d = x_prompt.shape
    db = x_sample.shape[0]
    assert x_sample.shape[1] == 1
    depth = w_in.shape[0]
    past = page_table.shape[1] * PAGE

    tabs_p = _rope_tables(jnp.arange(seq))
    tabs_s = _rope_tables(jnp.full((db,), past))
    head_major = lambda c: jnp.transpose(c, (0, 1, 3, 2, 4))
    c_flt = jnp.pad(jnp.swapaxes(cache_fox_logf, -1, -2), ((0, 0), (0, 0), (0, 8 - HA), (0, 0)))
    caches = (head_major(cache_fox_k), head_major(cache_fox_v), c_flt, head_major(cache_dsa_k),
              head_major(cache_dsa_v), jnp.swapaxes(cache_dsa_kidx, -1, -2), state_rwkv_wkv, state_rwkv_shift)

    xp = x_prompt.reshape(batch * seq, d)
    xs = x_sample.reshape(db, d)
    new_p, new_s = [], []
    for l in range(depth):
        final = l == depth - 1
        wz = _relayout_w_in(w_in[l])
        wo_bf = w_out[l].astype(BF16)
        bf_row = jnp.zeros((1, LANE), F32).at[0, M_F:M_F + HA].set(fox_forget_bias[l])
        rw_params = _rwkv_params(rwkv_mu[l], rwkv_w0[l], rwkv_w_lora_b[l], rwkv_a0[l], rwkv_a_lora_b[l],
                                 rwkv_k_k[l], rwkv_k_a[l], rwkv_r_k[l], rwkv_ln_w[l], rwkv_ln_b[l])
        xp, st_p = _layer_prompt(xp, batch, seq, norm_gain[l], wz, bf_row, tabs_p, rw_params, wo_bf,
                                 final_gain, final)
        xs, st_s = _layer_sample(xs, l, caches, page_table, norm_gain[l], wz, bf_row, tabs_s, rw_params,
                                 wo_bf, final_gain, final)
        new_p.append(st_p)
        new_s.append(st_s)
    outs_p = [jnp.stack(a) for a in zip(*new_p)]
    outs_s = [jnp.stack(a) for a in zip(*new_s)]
    return (xp.reshape(batch, seq, d), xs.reshape(db, 1, d), *outs_p, *outs_s)
```

```python
import functools

import jax
import jax.numpy as jnp
from jax import lax
from jax.experimental import pallas as pl
from jax.experimental.pallas import tpu as pltpu

F32 = jnp.float32
BF16 = jnp.bfloat16
HIGHEST = lax.Precision.HIGHEST

DH = 128
HA = 6
HB = 6
WA = HA * DH
WB = HB * DH
DC = 64
HC = 8
WC = HC * DC
HI = 16
DI = 64
TOPK = 256
ROPE_THETA = 500000.0
ROT_DIM = DH // 4
ROT_DIM_IDX = DI // 4
LORA = 32
PAGE = 128
RMS_EPS = 1e-6
GN_EPS = 64e-5
C_SHIFT = 3 * WC + 2 * LORA
LANE = 128
NEG = -1e30
INT_MIN = -(2 ** 31)
VMEM_LIMIT = 56 * 1024 * 1024

Z_FA_Q, Z_FA_K, Z_FA_V, Z_FA_G = 0, 768, 1536, 2304
Z_SB_V, Z_SB_G, Z_SB_Q, Z_SB_K = 3072, 3840, 4608, 5376
Z_IX_Q = 6144
Z_RC_G, Z_RC_R, Z_RC_K, Z_RC_V = 7168, 7680, 8192, 8704
Z_RC_WA = 9216
Z_MISC = 9344
NZ = 9728
M_IXK, M_IXW, M_F = 0, 64, 80
TN = 512
TILE_KINDS = ((0, 9, 0), (9, 12, 1), (12, 14, 2), (14, 18, 0), (18, 19, 3))
RC_CHUNK = 64
RC_ROWS = 128
INPROJ_TM = 1024
OUT_TM = 256
CUM_TM = 256
FLASH_TQ = 512
FLASH_RB = 128
FLASH_KB = 256
SEL_TQ = 256
PPS_ATTN = 16
PPS_IDX = 64


def _cparams(sem):
    return pltpu.CompilerParams(dimension_semantics=sem, vmem_limit_bytes=VMEM_LIMIT)


def _relayout_w_in(w):
    d = w.shape[0]
    wt = w.T
    o = {}
    off = 0
    for name, width in (('fa_q', WA), ('fa_k', WA), ('fa_v', WA), ('fa_f', HA), ('fa_g', WA),
                        ('sb_q', WB), ('sb_k', WB), ('sb_v', WB), ('sb_g', WB),
                        ('ix_q', HI * DI), ('ix_k', DI), ('ix_w', HI),
                        ('rc_shift', C_SHIFT), ('rc_g', WC)):
        o[name] = wt[off:off + width]
        off += width
    zeros = lambda n: jnp.zeros((n, d), w.dtype)
    rs = o['rc_shift']
    parts = [o['fa_q'], o['fa_k'], o['fa_v'], o['fa_g'], o['sb_v'], o['sb_g'], o['sb_q'], o['sb_k'],
             o['ix_q'], o['rc_g'], rs[:3 * WC], rs[3 * WC:], zeros(64),
             o['ix_k'], o['ix_w'], o['fa_f'], zeros(LANE - DI - HI - HA), zeros(NZ - Z_MISC - LANE)]
    wz = jnp.concatenate(parts, axis=0)
    assert wz.shape[0] == NZ
    return wz.astype(BF16)


def _rope_tables(pos):
    lane = jnp.arange(LANE)
    posf = pos.astype(F32)[:, None]

    def tab(head_w, rot, lane_limit):
        half = rot // 2
        lh = lane % head_w
        inv = ROPE_THETA ** (-(lh % half).astype(F32) * (2.0 / rot))
        ang = posf * inv[None, :]
        c, s = jnp.cos(ang), jnp.sin(ang)
        act = (lh < rot) & (lane < lane_limit)
        cos_t = jnp.where(act[None, :], c, 1.0)
        sin_lo = jnp.where((act & (lh >= half))[None, :], s, 0.0)
        sin_hi = jnp.where((act & (lh < half))[None, :], -s, 0.0)
        return [cos_t, sin_lo, sin_hi]

    return jnp.stack(tab(DH, ROT_DIM, LANE) + tab(DI, ROT_DIM_IDX, LANE) + tab(DI, ROT_DIM_IDX, DI), axis=0)


def _log_sigmoid(x):
    return jnp.minimum(x, 0.0) - jnp.log(1.0 + jnp.exp(-jnp.abs(x)))


def _rope_chunk(x, tab_ref, kind, half):
    c = tab_ref[3 * kind + 0]
    s_lo = tab_ref[3 * kind + 1]
    s_hi = tab_ref[3 * kind + 2]
    return x * c + pltpu.roll(x, half, 1) * s_lo + pltpu.roll(x, LANE - half, 1) * s_hi


def _inproj_kernel(x_ref, g_ref, w_ref, tab_ref, bf_ref, z_ref, xn_ref):
    j = pl.program_id(1)

    @pl.when(j == 0)
    def _():
        x = x_ref[...]
        ms = jnp.mean(x * x, axis=-1, keepdims=True)
        xn_ref[...] = (x * lax.rsqrt(ms + RMS_EPS) * g_ref[...]).astype(BF16)

    acc = lax.dot_general(xn_ref[...], w_ref[...], (((1,), (1,)), ((), ())), preferred_element_type=F32)
    n_chunk = TN // LANE
    for lo, hi, kind in TILE_KINDS:
        @pl.when((j >= lo) & (j < hi))
        def _(kind=kind):
            if kind == 0:
                z_ref[...] = acc
            elif kind in (1, 2):
                half = ROT_DIM // 2 if kind == 1 else ROT_DIM_IDX // 2
                for c in range(n_chunk):
                    sl = slice(c * LANE, (c + 1) * LANE)
                    z_ref[:, sl] = _rope_chunk(acc[:, sl], tab_ref, kind - 1, half)
            else:
                z_ref[...] = acc
                c = (Z_MISC - Z_RC_WA) // LANE
                sl = slice(c * LANE, (c + 1) * LANE)
                xm = acc[:, sl]
                roped = _rope_chunk(xm, tab_ref, 2, ROT_DIM_IDX // 2)
                lane = lax.broadcasted_iota(jnp.int32, xm.shape, 1)
                is_f = (lane >= M_F) & (lane < M_F + HA)
                z_ref[:, sl] = jnp.where(is_f, _log_sigmoid(xm + bf_ref[...]), roped)


def _inproj(x2d, gain, wz, tabs, bf_row, tm):
    rows, d = x2d.shape
    n_pos_blk = tabs.shape[1] // tm
    grid = (rows // tm, NZ // TN)
    return pl.pallas_call(
        _inproj_kernel,
        grid=grid,
        in_specs=[
            pl.BlockSpec((tm, d), lambda i, j: (i, 0)),
            pl.BlockSpec((1, d), lambda i, j: (0, 0)),
            pl.BlockSpec((TN, d), lambda i, j: (j, 0)),
            pl.BlockSpec((9, tm, LANE), lambda i, j: (0, i % n_pos_blk, 0)),
            pl.BlockSpec((1, LANE), lambda i, j: (0, 0)),
        ],
        out_specs=pl.BlockSpec((tm, TN), lambda i, j: (i, j)),
        out_shape=jax.ShapeDtypeStruct((rows, NZ), F32),
        scratch_shapes=[pltpu.VMEM((tm, d), BF16)],
        compiler_params=_cparams(("parallel", "arbitrary")),
    )(x2d, gain.reshape(1, d), wz, tabs, bf_row)


def _outproj_kernel(oa_ref, ob_ref, oc_ref, x_ref, w_ref, fg_ref, y_ref, *, final):
    acc = jnp.dot(oa_ref[...].astype(BF16), w_ref[0:WA, :], preferred_element_type=F32)
    acc += jnp.dot(ob_ref[...].astype(BF16), w_ref[WA:WA + WB, :], preferred_element_type=F32)
    acc += jnp.dot(oc_ref[...].astype(BF16), w_ref[WA + WB:, :], preferred_element_type=F32)
    y = x_ref[...] + acc
    if final:
        ms = jnp.mean(y * y, axis=-1, keepdims=True)
        y = y * lax.rsqrt(ms + RMS_EPS) * fg_ref[...]
    y_ref[...] = y


def _outproj(oa, ob, oc, x2d, w_bf, fgain, tm, final):
    rows, d = x2d.shape
    row_spec = lambda w: pl.BlockSpec((tm, w), lambda i: (i, 0))
    return pl.pallas_call(
        functools.partial(_outproj_kernel, final=final),
        grid=(rows // tm,),
        in_specs=[row_spec(WA), row_spec(WB), row_spec(WC), row_spec(d),
                  pl.BlockSpec((WA + WB + WC, d), lambda i: (0, 0)),
                  pl.BlockSpec((1, d), lambda i: (0, 0))],
        out_specs=row_spec(d),
        out_shape=jax.ShapeDtypeStruct((rows, d), F32),
        compiler_params=_cparams(("parallel",)),
    )(oa, ob, oc, x2d, w_bf, fgain.reshape(1, d))


def _cumsum_kernel(z_ref, cb_ref, carry_ref):
    @pl.when(pl.program_id(1) == 0)
    def _():
        carry_ref[...] = jnp.zeros_like(carry_ref)

    lf = z_ref[...]
    tm = lf.shape[0]
    r = lax.broadcasted_iota(jnp.int32, (tm, tm), 0)
    c = lax.broadcasted_iota(jnp.int32, (tm, tm), 1)
    tri = jnp.where(r >= c, 1.0, 0.0).astype(F32)
    cum = jnp.dot(tri, lf, precision=HIGHEST, preferred_element_type=F32) + carry_ref[...]
    carry_ref[...] = cum[tm - 1:tm, :]
    for h in range(HA):
        cb_ref[:, h * LANE:(h + 1) * LANE] = jnp.broadcast_to(cum[:, M_F + h:M_F + h + 1], (tm, LANE))


def _fox_cumsum(z, batch, seq, tm=256):
    nblk = seq // tm
    return pl.pallas_call(
        _cumsum_kernel,
        grid=(batch, nblk),
        in_specs=[pl.BlockSpec((tm, LANE), lambda b, i: (b * nblk + i, Z_MISC // LANE))],
        out_specs=pl.BlockSpec((tm, HA * LANE), lambda b, i: (b * nblk + i, 0)),
        out_shape=jax.ShapeDtypeStruct((batch * seq, HA * LANE), F32),
        scratch_shapes=[pltpu.VMEM((1, LANE), F32)],
        compiler_params=_cparams(("parallel", "arbitrary")),
    )(z)


def _silu(g):
    return g / (1.0 + jnp.exp(-g))


def _flash_kernel(*refs, mode, nh, tq):
    if mode == 'fox':
        q_ref, k_ref, v_ref, g_ref, cb_ref, o_ref, ko_ref, vo_ref, qs_ref, m_ref, l_ref, acc_ref = refs
    else:
        q_ref, k_ref, v_ref, g_ref, bias_ref, o_ref, ko_ref, vo_ref, qs_ref, m_ref, l_ref, acc_ref, bt_ref = refs
    qi = pl.program_id(1)
    ki = pl.program_id(2)
    nt = (((1,), (1,)), ((), ()))
    tn = (((0,), (0,)), ((), ()))

    @pl.when(ki == 0)
    def _():
        qs_ref[...] = (q_ref[...] * (DH ** -0.5)).astype(BF16)
        m_ref[...] = jnp.full_like(m_ref, -jnp.inf)
        l_ref[...] = jnp.zeros_like(l_ref)
        acc_ref[...] = jnp.zeros_like(acc_ref)

    def body(diag):
        if mode == 'dsa':
            bt_ref[...] = bias_ref[0].astype(F32)
        elif diag:
            key_i = lax.broadcasted_iota(jnp.int32, (tq, tq), 0)
            qry_i = lax.broadcasted_iota(jnp.int32, (tq, tq), 1)
            causal = key_i <= qry_i
        for h in range(nh):
            hs = slice(h * DH, (h + 1) * DH)
            st = lax.dot_general(k_ref[:, hs].astype(BF16), qs_ref[:, hs], nt, preferred_element_type=F32)
            if mode == 'fox':
                st = st - jnp.concatenate([cb_ref[:, hs]] * (tq // LANE), axis=1)
                if diag:
                    st = jnp.where(causal, st, NEG)
            else:
                st = st + bt_ref[...]
            m_prev = m_ref[h]
            m_new = jnp.maximum(m_prev, jnp.max(st, axis=0, keepdims=True))
            corr = jnp.exp(m_prev - m_new)
            p = jnp.exp(st - m_new)
            l_ref[h] = corr * l_ref[h] + jnp.sum(p, axis=0, keepdims=True)
            pv = lax.dot_general(v_ref[:, hs].astype(BF16), p.astype(BF16), tn, preferred_element_type=F32)
            acc_ref[hs, :] = corr * acc_ref[hs, :] + pv
            m_ref[h] = m_new

    @pl.when(ki < qi)
    def _():
        body(False)

    @pl.when(ki == qi)
    def _():
        body(True)
        for h in range(nh):
            hs = slice(h * DH, (h + 1) * DH)
            o_ref[:, hs] = (acc_ref[hs, :] / l_ref[h]).T * _silu(g_ref[:, hs])
            ko_ref[0, h] = k_ref[:, hs]
            vo_ref[0, h] = v_ref[:, hs]


def _flash(z, extra, batch, seq, mode, tq=512):
    nh = HA if mode == 'fox' else HB
    wgrp = nh * DH
    nq = seq // tq
    cq, ck, cv, cg = ((Z_FA_Q, Z_FA_K, Z_FA_V, Z_FA_G) if mode == 'fox' else (Z_SB_Q, Z_SB_K, Z_SB_V, Z_SB_G))
    qmap = lambda col: (lambda b, i, k: (b * nq + i, col // wgrp))
    kmap = lambda col: (lambda b, i, k: (b * nq + jnp.minimum(k, i), col // wgrp))
    in_specs = [pl.BlockSpec((tq, wgrp), qmap(cq)), pl.BlockSpec((tq, wgrp), kmap(ck)),
                pl.BlockSpec((tq, wgrp), kmap(cv)), pl.BlockSpec((tq, wgrp), qmap(cg))]
    scratch = [pltpu.VMEM((tq, wgrp), BF16), pltpu.VMEM((nh, 1, tq), F32), pltpu.VMEM((nh, 1, tq), F32),
               pltpu.VMEM((wgrp, tq), F32)]
    if mode == 'fox':
        in_specs += [pl.BlockSpec((tq, wgrp), lambda b, i, k: (b * nq + jnp.minimum(k, i), 0))]
    else:
        in_specs += [pl.BlockSpec((1, tq, tq), lambda b, i, k: (b, jnp.minimum(k, i), i))]
        scratch += [pltpu.VMEM((tq, tq), F32)]
    args = (z, z, z, z, extra)
    return pl.pallas_call(
        functools.partial(_flash_kernel, mode=mode, nh=nh, tq=tq),
        grid=(batch, nq, nq),
        in_specs=in_specs,
        out_specs=[pl.BlockSpec((tq, wgrp), lambda b, i, k: (b * nq + i, 0)),
                   pl.BlockSpec((1, nh, tq, DH), lambda b, i, k: (b, 0, i, 0)),
                   pl.BlockSpec((1, nh, tq, DH), lambda b, i, k: (b, 0, i, 0))],
        out_shape=[jax.ShapeDtypeStruct((batch * seq, wgrp), F32),
                   jax.ShapeDtypeStruct((batch, nh, seq, DH), F32),
                   jax.ShapeDtypeStruct((batch, nh, seq, DH), F32)],
        scratch_shapes=scratch,
        compiler_params=_cparams(("parallel", "parallel", "arbitrary")),
    )(*args)


def _sort_key(x):
    b = pltpu.bitcast(x + 0.0, jnp.int32)
    return b ^ ((b >> 31) & 0x7FFFFFFF)


def _kth_threshold(count_ge, shape, k):
    def step(it, u):
        bit = jnp.left_shift(jnp.int32(1), 31 - it)
        cand = u | bit
        cnt = count_ge(cand ^ INT_MIN)
        return jnp.where(cnt >= k, cand, u)

    u = lax.fori_loop(0, 32, step, jnp.zeros(shape, jnp.int32))
    return u ^ INT_MIN


def _dsa_select_kernel(qi_ref, qm_ref, km_ref, bias_ref, key_ref, qst_ref, *, tq, n_sel):
    i = pl.program_id(1)
    seq = km_ref.shape[0]
    n_kc = seq // tq
    nt = (((1,), (1,)), ((), ()))
    for h in range(HI):
        qst_ref[h] = qi_ref[...][:, h * DI:(h + 1) * DI].astype(BF16)
    wt = (qm_ref[...] * ((DI ** -0.5) * (HI ** -0.5))).T

    def score_chunk(c, _):
        c0 = pl.multiple_of(c * tq, tq)
        kc = km_ref[pl.ds(c0, tq), :][:, M_IXK:M_IXK + DI].astype(BF16)
        sc = jnp.zeros((tq, tq), F32)
        for h in range(HI):
            d = lax.dot_general(kc, qst_ref[h], nt, preferred_element_type=F32)
            sc = sc + jnp.maximum(d, 0.0) * wt[M_IXW + h:M_IXW + h + 1, :]
        key = _sort_key(sc)
        kr = lax.broadcasted_iota(jnp.int32, (tq, tq), 0)
        qc = lax.broadcasted_iota(jnp.int32, (tq, tq), 1)
        key_ref[pl.ds(c0, tq), :] = jnp.where((c < i) | (kr <= qc), key, INT_MIN)
        return 0

    lax.fori_loop(0, i + 1, score_chunk, 0)

    def count_ge(t):
        def chunk(c, acc):
            c0 = pl.multiple_of(c * tq, tq)
            hit = jnp.where(key_ref[pl.ds(c0, tq), :] >= t, 1.0, 0.0)
            return acc + jnp.sum(hit.reshape(tq // 8, 8, tq), axis=0)

        acc = lax.fori_loop(0, i + 1, chunk, jnp.zeros((8, tq), F32))
        return jnp.sum(acc, axis=0, keepdims=True)

    thr = _kth_threshold(count_ge, (1, tq), float(n_sel))

    def write_chunk(c, _):
        c0 = pl.multiple_of(c * tq, tq)
        kk = key_ref[pl.ds(c0, tq), :]
        sel = (kk >= thr) & (kk != INT_MIN)
        bias_ref[0, pl.ds(c0, tq), :] = jnp.where(sel, 0.0, NEG).astype(BF16)
        return 0

    def fill_chunk(c, _):
        c0 = pl.multiple_of(c * tq, tq)
        bias_ref[0, pl.ds(c0, tq), :] = jnp.full((tq, tq), NEG, BF16)
        return 0

    lax.fori_loop(0, i + 1, write_chunk, 0)
    lax.fori_loop(i + 1, n_kc, fill_chunk, 0)


def _dsa_select(z, batch, seq, tq=256):
    nq = seq // tq
    n_sel = min(TOPK, seq // 4)
    return pl.pallas_call(
        functools.partial(_dsa_select_kernel, tq=tq, n_sel=n_sel),
        grid=(batch, nq),
        in_specs=[pl.BlockSpec((tq, HI * DI), lambda b, i: (b * nq + i, Z_IX_Q // (HI * DI))),
                  pl.BlockSpec((tq, LANE), lambda b, i: (b * nq + i, Z_MISC // LANE)),
                  pl.BlockSpec((seq, LANE), lambda b, i: (b, Z_MISC // LANE))],
        out_specs=pl.BlockSpec((1, seq, tq), lambda b, i: (b, 0, i)),
        out_shape=jax.ShapeDtypeStruct((batch, seq, seq), BF16),
        scratch_shapes=[pltpu.VMEM((seq, tq), jnp.int32), pltpu.VMEM((HI, tq, DI), BF16)],
        compiler_params=_cparams(("parallel", "parallel")),
    )(z, z, z)


DIMS_NN = (((1,), (0,)), ((), ()))
DIMS_NT = (((1,), (1,)), ((), ()))
DIMS_TN = (((0,), (0,)), ((), ()))


def _dot_hi(a, b):
    return jnp.dot(a, b, precision=HIGHEST, preferred_element_type=F32)


def _split(x):
    hi = x.astype(BF16)
    lo = (x - hi.astype(F32)).astype(BF16)
    return hi, lo


def _dot3(a, b, dims=DIMS_NN):
    ah, al = a if isinstance(a, tuple) else _split(a)
    bh, bl = b if isinstance(b, tuple) else _split(b)
    d = lambda x, y: lax.dot_general(x, y, dims, preferred_element_type=F32)
    return d(ah, bh) + (d(ah, bl) + d(al, bh))


def _dot2(x, m_bf):
    hi, lo = _split(x)
    return (jnp.dot(hi, m_bf, preferred_element_type=F32) + jnp.dot(lo, m_bf, preferred_element_type=F32))


def _dot2l(m_bf, x):
    hi, lo = _split(x)
    return (jnp.dot(m_bf, hi, preferred_element_type=F32) + jnp.dot(m_bf, lo, preferred_element_type=F32))


def _rwkv_premix(cur, prev, p_ref, wb_ref, ab_ref, hones_ref):
    mix = lambda c, pv, mu: c + (pv - c) * mu
    xr = mix(cur['r'], prev['r'], p_ref[0:1, :])
    xk = mix(cur['k'], prev['k'], p_ref[1:2, :])
    xv = mix(cur['v'], prev['v'], p_ref[2:3, :])
    xwa = mix(cur['wa'], prev['wa'], p_ref[3:4, 0:LANE])
    hones = hones_ref[...]
    w_raw = p_ref[4:5, :] + _dot3(jnp.tanh(xwa), (wb_ref[0], wb_ref[1]))
    nw = -w_raw
    softplus = jnp.maximum(nw, 0.0) + jnp.log(1.0 + jnp.exp(-jnp.abs(nw)))
    lw = -jnp.exp(-softplus - 0.5)
    a = 1.0 / (1.0 + jnp.exp(-(p_ref[5:6, :] + _dot3(xwa, (ab_ref[0], ab_ref[1])))))
    kk0 = xk * p_ref[6:7, :]
    ss = _dot2(kk0 * kk0, hones)
    kk = kk0 / jnp.maximum(jnp.sqrt(ss), 1e-12)
    k2 = xk * (1.0 + (a - 1.0) * p_ref[7:8, :])
    bonus = _dot2(xr * k2 * p_ref[8:9, :], hones) * xv
    return xr, k2, xv, lw, kk, a, bonus


def _rwkv_finish(out, bonus, g, p_ref, hones_ref):
    hones = hones_ref[...]
    mean = _dot2(out, hones) * (1.0 / DC)
    d = out - mean
    var = _dot2(d * d, hones) * (1.0 / DC)
    y = d * lax.rsqrt(var + GN_EPS) * p_ref[9:10, :] + p_ref[10:11, :] + bonus
    return y * _silu(g)


def _rwkv_params(mu, w0, w_b, a0, a_b, k_k, k_a, r_k, ln_w, ln_b):
    pad = lambda v: jnp.concatenate([v, jnp.zeros((WC - v.shape[0],), F32)])
    rows = [mu[:WC], mu[WC:2 * WC], mu[2 * WC:3 * WC], pad(mu[3 * WC:]), w0, a0, k_k, k_a,
            r_k.reshape(WC), ln_w, ln_b]
    rows += [jnp.zeros((WC,), F32)] * (16 - len(rows))
    p = jnp.stack(rows, axis=0)
    split2 = lambda m: jnp.stack(_split(m), axis=0)
    wb = split2(jnp.zeros((LANE, WC), F32).at[0:LORA].set(w_b))
    ab = split2(jnp.zeros((LANE, WC), F32).at[LORA:2 * LORA].set(a_b))
    hid = jnp.arange(WC) // DC
    hones = (hid[:, None] == hid[None, :]).astype(BF16)
    return p, wb, ab, hones


def _rwkv_chunk_terms(units, first, strict, incl, eye):
    C = units[0][0].shape[0]
    cat0 = lambda xs: jnp.concatenate(xs, axis=0)
    cat1 = lambda xs: jnp.concatenate(xs, axis=1)
    stack = lambda x: cat0([jnp.where(first, x, 0.0), jnp.where(first, 0.0, x)])
    dup = lambda x: cat0([x, x])
    la = [stack(u[0]) for u in units]
    lr = [stack(u[1]) for u in units]
    vs = [stack(u[6]) for u in units]
    xx = [_dot3(cat0([a, r]), cat0([dup(u[2]), dup(u[3])]), DIMS_NT) for a, r, u in zip(la, lr, units)]
    n_m = [jnp.where(strict, x[:2 * C, :2 * C], 0.0) for x in xx]
    m_m = [jnp.where(strict, x[:2 * C, 2 * C:], 0.0) for x in xx]
    pb = [jnp.where(incl, x[2 * C:, :2 * C], 0.0) for x in xx]
    pk = [jnp.where(incl, x[2 * C:, 2 * C:], 0.0) for x in xx]
    t_inv = [jnp.where(eye, 1.0, 0.0) + n for n in n_m]
    pw_s = [_split(n) for n in n_m]
    sz = 2
    while sz < C:
        pw_s = [_split(_dot3(s, s)) for s in pw_s]
        t_inv = [t + _dot3(s, t) for s, t in zip(pw_s, t_inv)]
        sz *= 2
    mv = [_dot3(m, v) for m, v in zip(m_m, vs)]
    ta = [_dot3(t, cat1([a, m])) for t, a, m in zip(t_inv, la, mv)]
    big = [_split(cat0([t, cat1([jnp.zeros_like(v), v])])) for t, v in zip(ta, vs)]
    pp = [_dot3(cat1([b, k]), g) for b, k, g in zip(pb, pk, big)]
    php = [_dot3(cat0([stack(u[4]), stack(u[5])]), g, DIMS_TN) for u, g in zip(units, big)]
    terms = []
    for u, r, p, h in zip(units, lr, pp, php):
        r1s = r + p[:, :LANE]
        ovs = p[:, LANE:]
        phi = jnp.where(eye, jnp.broadcast_to(u[7], (LANE, LANE)), 0.0) + h[:, :LANE]
        terms.append((cat0([phi, r1s[:C] + r1s[C:]]), ovs[:C] + ovs[C:], h[:, LANE:]))
    return terms


def _rwkv_chunk_kernel(r_ref, k_ref, v_ref, wa_ref, g_ref, p_ref, wb_ref, ab_ref, hones_ref,
                       o_ref, st_ref, s_ref, pr_ref, pk_ref, pv_ref, pwa_ref):
    step = pl.program_id(1)
    R = r_ref.shape[0]
    C = RC_CHUNK
    ncs = R // C

    @pl.when(step == 0)
    def _():
        s_ref[...] = jnp.zeros_like(s_ref)
        pr_ref[...] = jnp.zeros_like(pr_ref)
        pk_ref[...] = jnp.zeros_like(pk_ref)
        pv_ref[...] = jnp.zeros_like(pv_ref)
        pwa_ref[...] = jnp.zeros_like(pwa_ref)

    def shifted(x, carry_ref):
        row = lax.broadcasted_iota(jnp.int32, x.shape, 0)
        pv = jnp.where(row == 0, carry_ref[...], pltpu.roll(x, 1, 0))
        carry_ref[...] = x[R - 1:R, :]
        return pv

    cur = {'r': r_ref[...], 'k': k_ref[...], 'v': v_ref[...], 'wa': wa_ref[...]}
    prev = {'r': shifted(cur['r'], pr_ref), 'k': shifted(cur['k'], pk_ref),
            'v': shifted(cur['v'], pv_ref), 'wa': shifted(cur['wa'], pwa_ref)}
    xr, k2, xv, lw, kk, a, bonus = _rwkv_premix(cur, prev, p_ref, wb_ref, ab_ref, hones_ref)

    ri = lax.broadcasted_iota(jnp.int32, (R, R), 0)
    ci = lax.broadcasted_iota(jnp.int32, (R, R), 1)
    tril = jnp.where((ri >= ci) & ((ri // C) == (ci // C)), 1.0, 0.0).astype(BF16)
    lc = _dot2l(tril, lw)
    lend = jnp.concatenate([jnp.broadcast_to(lc[(c + 1) * C - 1:(c + 1) * C, :], (C, WC)) for c in range(ncs)],
                           axis=0)
    e_neg = jnp.exp(-lc)
    e_end = jnp.exp(lend - lc)
    bn = kk * a
    a_t = -kk * jnp.exp(lc - lw)
    r_t = xr * jnp.exp(lc)
    b_t = bn * e_neg
    k_t = k2 * e_neg
    b_h = bn * e_end
    k_h = k2 * e_end

    first = lax.broadcasted_iota(jnp.int32, (1, LANE), 1) < DC
    pr_i = lax.broadcasted_iota(jnp.int32, (LANE, LANE), 0)
    pc_i = lax.broadcasted_iota(jnp.int32, (LANE, LANE), 1)
    same = (pr_i // DC) == (pc_i // DC)
    strict = same & (pr_i > pc_i)
    incl = same & (pr_i >= pc_i)
    eye = pr_i == pc_i

    units = []
    for c in range(ncs):
        rs = slice(c * C, (c + 1) * C)
        g_row = jnp.exp(lc[(c + 1) * C - 1:(c + 1) * C, :])
        for p in range(HC // 2):
            ps = slice(p * LANE, (p + 1) * LANE)
            units.append((a_t[rs, ps], r_t[rs, ps], b_t[rs, ps], k_t[rs, ps], b_h[rs, ps], k_h[rs, ps],
                          xv[rs, ps], g_row[:, ps]))
    terms = _rwkv_chunk_terms(units, first, strict, incl, eye)

    states = [s_ref[p] for p in range(HC // 2)]
    out_rows = []
    for c in range(ncs):
        outs = []
        for p in range(HC // 2):
            lhs, ov, psi = terms[c * (HC // 2) + p]
            sr = _dot3(lhs, states[p])
            outs.append(sr[LANE:] + ov)
            states[p] = sr[:LANE] + psi
        out_rows.append(jnp.concatenate(outs, axis=1))
    out = jnp.concatenate(out_rows, axis=0)
    o_ref[...] = _rwkv_finish(out, bonus, g_ref[...], p_ref, hones_ref)
    for p in range(HC // 2):
        s_ref[p] = states[p]

    @pl.when(step == pl.num_programs(1) - 1)
    def _():
        for p in range(HC // 2):
            st_ref[0, p] = states[p]


def _rwkv_prompt(z, params, batch, seq):
    p, wb, ab, hones = params
    R = min(RC_ROWS, seq)
    nc = seq // R
    col = lambda off, w: (lambda b, i: (b * nc + i, off // w))
    const = lambda shp: pl.BlockSpec(shp, lambda b, i: (0,) * len(shp))
    o, st = pl.pallas_call(
        _rwkv_chunk_kernel,
        grid=(batch, nc),
        in_specs=[pl.BlockSpec((R, WC), col(Z_RC_R, WC)), pl.BlockSpec((R, WC), col(Z_RC_K, WC)),
                  pl.BlockSpec((R, WC), col(Z_RC_V, WC)), pl.BlockSpec((R, LANE), col(Z_RC_WA, LANE)),
                  pl.BlockSpec((R, WC), col(Z_RC_G, WC)),
                  const((16, WC)), const((2, LANE, WC)), const((2, LANE, WC)), const((WC, WC))],
        out_specs=[pl.BlockSpec((R, WC), lambda b, i: (b * nc + i, 0)),
                   pl.BlockSpec((1, HC // 2, LANE, LANE), lambda b, i: (b, 0, 0, 0))],
        out_shape=[jax.ShapeDtypeStruct((batch * seq, WC), F32),
                   jax.ShapeDtypeStruct((batch, HC // 2, LANE, LANE), F32)],
        scratch_shapes=[pltpu.VMEM((HC // 2, LANE, LANE), F32), pltpu.VMEM((1, WC), F32),
                        pltpu.VMEM((1, WC), F32), pltpu.VMEM((1, WC), F32), pltpu.VMEM((1, LANE), F32)],
        compiler_params=_cparams(("parallel", "arbitrary")),
    )(z, z, z, z, z, p, wb, ab, hones)
    st = st.reshape(batch, HC // 2, 2, DC, 2, DC)
    st = jnp.stack([st[:, :, 0, :, 0, :], st[:, :, 1, :, 1, :]], axis=2).reshape(batch, HC, DC, DC)
    return o, jnp.swapaxes(st, -1, -2)


def _paged_attn_kernel(*refs, mode, pps, nh):
    pt_ref = refs[0]
    q_ref, kn_ref, vn_ref, g_ref = refs[1:5]
    k_refs = refs[5:5 + pps]
    v_refs = refs[5 + pps:5 + 2 * pps]
    rest = refs[5 + 2 * pps:]
    if mode == 'fox':
        lf_refs = rest[:pps]
        lfn_ref = rest[pps]
        o_ref, qm_ref, m_ref, l_ref, acc_ref, carry_ref = rest[pps + 1:]
    else:
        bias_ref, biasn_ref = rest[:2]
        o_ref, qm_ref, m_ref, l_ref, acc_ref = rest[2:]
    del pt_ref
    step = pl.program_id(1)
    w = nh * DH
    hrow = lax.broadcasted_iota(jnp.int32, (8, w), 0)
    hcol = lax.broadcasted_iota(jnp.int32, (8, w), 1) // DH
    own = hrow == hcol

    @pl.when(step == 0)
    def _():
        qm_ref[...] = jnp.where(own, q_ref[0] * (DH ** -0.5), 0.0).astype(BF16)
        m_ref[...] = jnp.full_like(m_ref, -jnp.inf)
        l_ref[...] = jnp.zeros_like(l_ref)
        acc_ref[...] = jnp.zeros_like(acc_ref)
        if mode == 'fox':
            carry_ref[...] = jnp.zeros_like(carry_ref)

    nt = (((1,), (1,)), ((), ()))
    qm = qm_ref[...]
    s_parts = []
    if mode == 'fox':
        ur = lax.broadcasted_iota(jnp.int32, (LANE, LANE), 0)
        uc = lax.broadcasted_iota(jnp.int32, (LANE, LANE), 1)
        upper = jnp.where(ur <= uc, 1.0, 0.0).astype(F32)
        carry = carry_ref[...]
    cat = lambda ref: jnp.concatenate([ref[h] for h in range(nh)], axis=1).astype(BF16)
    for i in range(pps):
        s = lax.dot_general(qm, cat(k_refs[i]), nt, preferred_element_type=F32)
        if mode == 'fox':
            cum = _dot_hi(lf_refs[i][...], upper) + carry
            carry = jnp.broadcast_to(cum[:, LANE - 1:LANE], (8, LANE))
            s = s - cum
        else:
            s = s + bias_ref[0, :, i * LANE:(i + 1) * LANE]
        s_parts.append(s)
    if mode == 'fox':
        carry_ref[...] = carry
    s_all = jnp.concatenate(s_parts, axis=1)
    m_prev = m_ref[...]
    m_new = jnp.maximum(m_prev, jnp.max(s_all, axis=1, keepdims=True))
    corr = jnp.exp(m_prev - m_new)
    p_all = jnp.exp(s_all - m_new)
    l_ref[...] = corr * l_ref[...] + jnp.sum(p_all, axis=1, keepdims=True)
    pv = jnp.zeros((8, w), F32)
    for i in range(pps):
        pv = pv + jnp.dot(p_all[:, i * LANE:(i + 1) * LANE].astype(BF16), cat(v_refs[i]),
                          preferred_element_type=F32)
    acc_ref[...] = corr * acc_ref[...] + pv
    m_ref[...] = m_new

    @pl.when(step == pl.num_programs(1) - 1)
    def _():
        kn = kn_ref[0].astype(BF16).astype(F32)
        sn = jnp.sum(qm_ref[...].astype(F32) * kn, axis=1, keepdims=True)
        if mode == 'fox':
            sn = sn - (carry_ref[:, 0:1] + lfn_ref[0])
        else:
            sn = sn + biasn_ref[0, :, 0:1]
        m_p = m_ref[...]
        m_n = jnp.maximum(m_p, sn)
        cr = jnp.exp(m_p - m_n)
        pn = jnp.exp(sn - m_n)
        lsum = cr * l_ref[...] + pn
        acc = cr * acc_ref[...] + pn * vn_ref[0]
        o = jnp.sum(jnp.where(own, acc / lsum, 0.0), axis=0, keepdims=True)
        o_ref[0] = o * _silu(g_ref[0])


def _paged_attn(zs3, cache_k, cache_v, layer, page_table, extra, mode):
    db = zs3.shape[0]
    n_pages = page_table.shape[1]
    pps = min(PPS_ATTN, n_pages)
    nsteps = n_pages // pps
    nh = HA if mode == 'fox' else HB
    w = nh * DH
    cq, ck, cv, cg = ((Z_FA_Q, Z_FA_K, Z_FA_V, Z_FA_G) if mode == 'fox' else (Z_SB_Q, Z_SB_K, Z_SB_V, Z_SB_G))
    zspec = lambda col: pl.BlockSpec((1, 1, w), lambda b, s, pt: (b, 0, col // w))
    page_map = lambda i: (lambda b, s, pt: (layer, pt[b * n_pages + s * pps + i], 0, 0))
    page_map5 = lambda i: (lambda b, s, pt: (layer, pt[b * n_pages + s * pps + i], 0, 0, 0))
    in_specs = [zspec(cq), zspec(ck), zspec(cv), zspec(cg)]
    in_specs += [pl.BlockSpec((None, None, nh, PAGE, DH), page_map5(i)) for i in range(pps)]
    in_specs += [pl.BlockSpec((None, None, nh, PAGE, DH), page_map5(i)) for i in range(pps)]
    args = [zs3, zs3, zs3, zs3] + [cache_k] * pps + [cache_v] * pps
    scratch = [pltpu.VMEM((8, w), BF16), pltpu.VMEM((8, 1), F32), pltpu.VMEM((8, 1), F32),
               pltpu.VMEM((8, w), F32)]
    if mode == 'fox':
        logf_t, lf_new = extra
        in_specs += [pl.BlockSpec((None, None, 8, PAGE), page_map(i)) for i in range(pps)]
        in_specs += [pl.BlockSpec((1, 8, 1), lambda b, s, pt: (b, 0, 0))]
        args += [logf_t] * pps + [lf_new]
        scratch += [pltpu.VMEM((8, LANE), F32)]
    else:
        bias = extra
        in_specs += [pl.BlockSpec((1, 1, pps * LANE), lambda b, s, pt: (b, 0, s)),
                     pl.BlockSpec((1, 1, LANE), lambda b, s, pt: (b, 0, n_pages))]
        args += [bias, bias]
    grid_spec = pltpu.PrefetchScalarGridSpec(
        num_scalar_prefetch=1, grid=(db, nsteps), in_specs=in_specs,
        out_specs=pl.BlockSpec((1, 1, w), lambda b, s, pt: (b, 0, 0)),
        scratch_shapes=scratch)
    return pl.pallas_call(
        functools.partial(_paged_attn_kernel, mode=mode, pps=pps, nh=nh),
        grid_spec=grid_spec,
        out_shape=jax.ShapeDtypeStruct((db, 1, w), F32),
        compiler_params=_cparams(("parallel", "arbitrary")),
    )(page_table.reshape(-1), *args)


def _sample_select_kernel(*refs, pps, n_sel):
    pt_ref = refs[0]
    qi_ref, wi_ref, kin_ref = refs[1:4]
    ki_refs = refs[4:4 + pps]
    bias_ref, key_ref = refs[4 + pps:]
    del pt_ref
    step = pl.program_id(1)
    nsteps = pl.num_programs(1)
    past = key_ref.shape[1] - LANE
    qb = qi_ref[0].astype(BF16)
    w = wi_ref[0] * ((DI ** -0.5) * (HI ** -0.5))

    def score(kmat_t):
        d = jnp.dot(qb, kmat_t.astype(BF16), preferred_element_type=F32)
        return jnp.sum(jnp.maximum(d, 0.0) * w, axis=0, keepdims=True)

    for i in range(pps):
        off = pl.multiple_of((step * pps + i) * LANE, LANE)
        key_ref[:, pl.ds(off, LANE)] = _sort_key(score(ki_refs[i][...]))

    @pl.when(step == nsteps - 1)
    def _():
        dn = jnp.sum(qb.astype(F32) * kin_ref[0].astype(BF16).astype(F32), axis=1, keepdims=True)
        sn = jnp.sum(jnp.maximum(dn, 0.0) * w, axis=0, keepdims=True)
        lane = lax.broadcasted_iota(jnp.int32, (1, LANE), 1)
        key_ref[:, past:past + LANE] = jnp.where(lane == 0, _sort_key(jnp.broadcast_to(sn, (1, LANE))), INT_MIN)
        keys = key_ref[...]

        def count_ge(t):
            return jnp.sum(jnp.where(keys >= t, 1.0, 0.0), axis=1, keepdims=True)

        thr = _kth_threshold(count_ge, (1, 1), float(n_sel))
        sel = (keys >= thr) & (keys != INT_MIN)
        bias_ref[0] = jnp.where(sel, 0.0, NEG)


def _sample_select(qi3, wi3, kin3, cache_ki, layer, page_table):
    db = qi3.shape[0]
    n_pages = page_table.shape[1]
    pps = min(PPS_IDX, n_pages)
    past = n_pages * PAGE
    n_sel = min(TOPK, (past + 1) // 4)
    page_map = lambda i: (lambda b, s, pt: (layer, pt[b * n_pages + s * pps + i], 0, 0))
    in_specs = [pl.BlockSpec((1, HI, DI), lambda b, s, pt: (b, 0, 0)),
                pl.BlockSpec((1, HI, 1), lambda b, s, pt: (b, 0, 0)),
                pl.BlockSpec((1, 1, DI), lambda b, s, pt: (b, 0, 0))]
    in_specs += [pl.BlockSpec((None, None, DI, PAGE), page_map(i)) for i in range(pps)]
    grid_spec = pltpu.PrefetchScalarGridSpec(
        num_scalar_prefetch=1, grid=(db, n_pages // pps), in_specs=in_specs,
        out_specs=pl.BlockSpec((1, 1, past + LANE), lambda b, s, pt: (b, 0, 0)),
        scratch_shapes=[pltpu.VMEM((1, past + LANE), jnp.int32)])
    return pl.pallas_call(
        functools.partial(_sample_select_kernel, pps=pps, n_sel=n_sel),
        grid_spec=grid_spec,
        out_shape=jax.ShapeDtypeStruct((db, 1, past + LANE), F32),
        compiler_params=_cparams(("parallel", "arbitrary")),
    )(page_table.reshape(-1), qi3, wi3, kin3, *([cache_ki] * pps))


def _rwkv_pre_kernel(r_ref, k_ref, v_ref, wa_ref, sr_ref, sk_ref, sv_ref, swa_ref, p_ref, wb_ref, ab_ref,
                     hones_ref, or_ref, ok_ref, ov_ref, ow_ref, onk_ref, oka_ref, ob_ref):
    cur = {'r': r_ref[...], 'k': k_ref[...], 'v': v_ref[...], 'wa': wa_ref[...]}
    prev = {'r': sr_ref[...], 'k': sk_ref[...], 'v': sv_ref[...], 'wa': swa_ref[...]}
    xr, k2, xv, lw, kk, a, bonus = _rwkv_premix(cur, prev, p_ref, wb_ref, ab_ref, hones_ref)
    or_ref[...] = xr
    ok_ref[...] = k2
    ov_ref[...] = xv
    ow_ref[...] = jnp.exp(lw)
    onk_ref[...] = -kk
    oka_ref[...] = kk * a
    ob_ref[...] = bonus


def _rwkv_step_kernel(s_ref, r_ref, k_ref, v_ref, w_ref, nk_ref, ka_ref, so_ref, o_ref):
    s = s_ref[...]
    sa = jnp.sum(s * nk_ref[...], axis=-1, keepdims=True)
    s2 = s * w_ref[...] + sa * ka_ref[...] + v_ref[...] * k_ref[...]
    so_ref[...] = s2
    o_ref[...] = jnp.sum(s2 * r_ref[...], axis=-1, keepdims=True)


def _rwkv_post_kernel(o_ref, b_ref, g_ref, p_ref, hones_ref, y_ref):
    y_ref[...] = _rwkv_finish(o_ref[...], b_ref[...], g_ref[...], p_ref, hones_ref)


def _rwkv_sample(zs, shift_prev, wkv, params, bb=8):
    p, wb, ab, hones = params
    db = zs.shape[0]
    full = lambda a: pl.BlockSpec(a.shape, lambda i: (0,) * a.ndim)
    zcol = lambda off, w: pl.BlockSpec((db, w), lambda i: (0, off // w))
    swa = jnp.concatenate([shift_prev[:, 3 * WC:], jnp.zeros((db, LANE - 2 * LORA), F32)], axis=1)
    sr, sk, sv = shift_prev[:, :WC], shift_prev[:, WC:2 * WC], shift_prev[:, 2 * WC:3 * WC]
    rows = pl.pallas_call(
        _rwkv_pre_kernel,
        grid=(1,),
        in_specs=[zcol(Z_RC_R, WC), zcol(Z_RC_K, WC), zcol(Z_RC_V, WC), zcol(Z_RC_WA, LANE),
                  full(sr), full(sk), full(sv), full(swa), full(p), full(wb), full(ab), full(hones)],
        out_specs=[pl.BlockSpec((db, WC), lambda i: (0, 0))] * 7,
        out_shape=[jax.ShapeDtypeStruct((db, WC), F32)] * 7,
        compiler_params=_cparams(("arbitrary",)),
    )(zs, zs, zs, zs, sr, sk, sv, swa, p, wb, ab, hones)
    xr, k2, xv, dec, nkk, kka, bonus = rows
    as_row = lambda t: t.reshape(db, HC, 1, DC)
    rspec = pl.BlockSpec((bb, HC, 1, DC), lambda i: (i, 0, 0, 0))
    cspec = pl.BlockSpec((bb, HC, DC, 1), lambda i: (i, 0, 0, 0))
    sspec = pl.BlockSpec((bb, HC, DC, DC), lambda i: (i, 0, 0, 0))
    s_new, o_col = pl.pallas_call(
        _rwkv_step_kernel,
        grid=(db // bb,),
        in_specs=[sspec, rspec, rspec, cspec, rspec, rspec, rspec],
        out_specs=[sspec, cspec],
        out_shape=[jax.ShapeDtypeStruct((db, HC, DC, DC), F32), jax.ShapeDtypeStruct((db, HC, DC, 1), F32)],
        compiler_params=_cparams(("parallel",)),
    )(wkv, as_row(xr), as_row(k2), xv.reshape(db, HC, DC, 1), as_row(dec), as_row(nkk), as_row(kka))
    o_rows = o_col.reshape(db, WC)
    y = pl.pallas_call(
        _rwkv_post_kernel,
        grid=(1,),
        in_specs=[full(o_rows), full(bonus), zcol(Z_RC_G, WC), full(p), full(hones)],
        out_specs=pl.BlockSpec((db, WC), lambda i: (0, 0)),
        out_shape=jax.ShapeDtypeStruct((db, WC), F32),
        compiler_params=_cparams(("arbitrary",)),
    )(o_rows, bonus, zs, p, hones)
    return y, s_new


def _shift_cols(z_last):
    return jnp.concatenate([z_last[..., Z_RC_R:Z_RC_R + 3 * WC], z_last[..., Z_RC_WA:Z_RC_WA + 2 * LORA]], axis=-1)


def _layer_prompt(x2d, batch, seq, gain, wz, bf_row, tabs, rw_params, wo_bf, fgain, final):
    z = _inproj(x2d, gain, wz, tabs, bf_row, tm=min(INPROJ_TM, seq))
    c_t = _fox_cumsum(z, batch, seq, tm=min(CUM_TM, seq))
    o_a, fk, fv = _flash(z, c_t, batch, seq, 'fox', tq=min(FLASH_TQ, seq))
    bias = _dsa_select(z, batch, seq, tq=min(SEL_TQ, seq))
    o_b, dk, dv = _flash(z, bias, batch, seq, 'dsa', tq=min(FLASH_TQ, seq))
    o_c, wkv = _rwkv_prompt(z, rw_params, batch, seq)
    y = _outproj(o_a, o_b, o_c, x2d, wo_bf, fgain, tm=min(OUT_TM, seq), final=final)
    z3 = z.reshape(batch, seq, NZ)
    st = (fk, fv, z3[..., Z_MISC + M_F:Z_MISC + M_F + HA], dk, dv,
          z3[..., Z_MISC + M_IXK:Z_MISC + M_IXK + DI], wkv, _shift_cols(z3[:, -1]))
    return y, st


def _layer_sample(xs2d, layer, caches, page_table, gain, wz, bf_row, tabs, rw_params, wo_bf, fgain, final):
    c_fk, c_fv, c_flt, c_dk, c_dv, c_di, s_wkv, s_shift = caches
    db = xs2d.shape[0]
    zs = _inproj(xs2d, gain, wz, tabs, bf_row, tm=db)
    zs3 = zs.reshape(db, 1, NZ)
    misc = zs[:, Z_MISC:Z_MISC + LANE]
    lf_new = jnp.concatenate([misc[:, M_F:M_F + HA], jnp.zeros((db, 8 - HA), F32)], axis=1).reshape(db, 8, 1)
    o_a = _paged_attn(zs3, c_fk, c_fv, layer, page_table, (c_flt, lf_new), 'fox')
    qi3 = zs[:, Z_IX_Q:Z_IX_Q + HI * DI].reshape(db, HI, DI)
    wi3 = misc[:, M_IXW:M_IXW + HI].reshape(db, HI, 1)
    kin3 = misc[:, M_IXK:M_IXK + DI].reshape(db, 1, DI)
    bias = _sample_select(qi3, wi3, kin3, c_di, layer, page_table)
    o_b = _paged_attn(zs3, c_dk, c_dv, layer, page_table, bias, 'dsa')
    o_c, wkv = _rwkv_sample(zs, s_shift[layer], s_wkv[layer], rw_params)
    y = _outproj(o_a.reshape(db, WA), o_b.reshape(db, WB), o_c, xs2d, wo_bf, fgain, tm=db, final=final)
    st = (zs[:, Z_FA_K:Z_FA_K + WA].reshape(db, 1, HA, DH),
          zs[:, Z_FA_V:Z_FA_V + WA].reshape(db, 1, HA, DH),
          misc[:, M_F:M_F + HA].reshape(db, 1, HA),
          zs[:, Z_SB_K:Z_SB_K + WB].reshape(db, 1, HB, DH),
          zs[:, Z_SB_V:Z_SB_V + WB].reshape(db, 1, HB, DH),
          misc[:, M_IXK:M_IXK + DI].reshape(db, 1, DI),
          wkv, _shift_cols(zs))
    return y, st


def kernel(x_prompt, x_sample, cache_fox_k, cache_fox_v, cache_fox_logf, cache_dsa_k, cache_dsa_v, cache_dsa_kidx, state_rwkv_wkv, state_rwkv_shift, page_table, norm_gain, w_in, fox_forget_bias, rwkv_mu, rwkv_w0, rwkv_w_lora_b, rwkv_a0, rwkv_a_lora_b, rwkv_k_k, rwkv_k_a, rwkv_r_k, rwkv_ln_w, rwkv_ln_b, w_out, final_gain):
    batch, seq, d = x_prompt.shape
    db = x_sample.shape[0]
    assert x_sample.shape[1] == 1
    depth = w_in.shape[0]
    past = page_table.shape[1] * PAGE

    tabs_p = _rope_tables(jnp.arange(seq))
    tabs_s = _rope_tables(jnp.full((db,), past))
    head_major = lambda c: jnp.transpose(c, (0, 1, 3, 2, 4))
    c_flt = jnp.pad(jnp.swapaxes(cache_fox_logf, -1, -2), ((0, 0), (0, 0), (0, 8 - HA), (0, 0)))
    caches = (head_major(cache_fox_k), head_major(cache_fox_v), c_flt, head_major(cache_dsa_k),
              head_major(cache_dsa_v), jnp.swapaxes(cache_dsa_kidx, -1, -2), state_rwkv_wkv, state_rwkv_shift)

    xp = x_prompt.reshape(batch * seq, d)
    xs = x_sample.reshape(db, d)
    new_p, new_s = [], []
    for l in range(depth):
        final = l == depth - 1
        wz = _relayout_w_in(w_in[l])
        wo_bf = w_out[l].astype(BF16)
        bf_row = jnp.zeros((1, LANE), F32).at[0, M_F:M_F + HA].set(fox_forget_bias[l])
        rw_params = _rwkv_params(rwkv_mu[l], rwkv_w0[l], rwkv_w_lora_b[l], rwkv_a0[l], rwkv_a_lora_b[l],
                                 rwkv_k_k[l], rwkv_k_a[l], rwkv_r_k[l], rwkv_ln_w[l], rwkv_ln_b[l])
        xp, st_p = _layer_prompt(xp, batch, seq, norm_gain[l], wz, bf_row, tabs_p, rw_params, wo_bf,
                                 final_gain, final)
        xs, st_s = _layer_sample(xs, l, caches, page_table, norm_gain[l], wz, bf_row, tabs_s, rw_params,
                                 wo_bf, final_gain, final)
        new_p.append(st_p)
        new_s.append(st_s)
    outs_p = [jnp.stack(a) for a in zip(*new_p)]
    for i in (0, 1, 3, 4):
        outs_p[i] = jnp.transpose(outs_p[i], (0, 1, 3, 2, 4))
    outs_s = [jnp.stack(a) for a in zip(*new_s)]
    return (xp.reshape(batch, seq, d), xs.reshape(db, 1, d), *outs_p, *outs_s)
```

```python
import functools

import jax
import jax.numpy as jnp
from jax import lax
from jax.experimental import pallas as pl
from jax.experimental.pallas import tpu as pltpu

F32 = jnp.float32
BF16 = jnp.bfloat16
HIGHEST = lax.Precision.HIGHEST

DH = 128
HA = 6
HB = 6
WA = HA * DH
WB = HB * DH
DC = 64
HC = 8
WC = HC * DC
HI = 16
DI = 64
TOPK = 256
ROPE_THETA = 500000.0
ROT_DIM = DH // 4
ROT_DIM_IDX = DI // 4
LORA = 32
PAGE = 128
RMS_EPS = 1e-6
GN_EPS = 64e-5
C_SHIFT = 3 * WC + 2 * LORA
LANE = 128
NEG = -1e30
INT_MIN = -(2 ** 31)
VMEM_LIMIT = 56 * 1024 * 1024

Z_FA_Q, Z_FA_K, Z_FA_V, Z_FA_G = 0, 768, 1536, 2304
Z_SB_V, Z_SB_G, Z_SB_Q, Z_SB_K = 3072, 3840, 4608, 5376
Z_IX_Q = 6144
Z_RC_G, Z_RC_R, Z_RC_K, Z_RC_V = 7168, 7680, 8192, 8704
Z_RC_WA = 9216
Z_MISC = 9344
NZ = 9728
M_IXK, M_IXW, M_F = 0, 64, 80
TN = 512
TILE_KINDS = ((0, 9, 0), (9, 12, 1), (12, 14, 2), (14, 18, 0), (18, 19, 3))
RC_CHUNK = 64
RC_ROWS = 128
INPROJ_TM = 1024
OUT_TM = 256
CUM_TM = 256
FLASH_TQ = 512
FLASH_RB = 128
FLASH_KB = 256
SEL_TQ = 256
PPS_ATTN = 16
PPS_IDX = 64


def _cparams(sem):
    return pltpu.CompilerParams(dimension_semantics=sem, vmem_limit_bytes=VMEM_LIMIT)


def _relayout_w_in(w):
    d = w.shape[0]
    wt = w.T
    o = {}
    off = 0
    for name, width in (('fa_q', WA), ('fa_k', WA), ('fa_v', WA), ('fa_f', HA), ('fa_g', WA),
                        ('sb_q', WB), ('sb_k', WB), ('sb_v', WB), ('sb_g', WB),
                        ('ix_q', HI * DI), ('ix_k', DI), ('ix_w', HI),
                        ('rc_shift', C_SHIFT), ('rc_g', WC)):
        o[name] = wt[off:off + width]
        off += width
    zeros = lambda n: jnp.zeros((n, d), w.dtype)
    rs = o['rc_shift']
    parts = [o['fa_q'], o['fa_k'], o['fa_v'], o['fa_g'], o['sb_v'], o['sb_g'], o['sb_q'], o['sb_k'],
             o['ix_q'], o['rc_g'], rs[:3 * WC], rs[3 * WC:], zeros(64),
             o['ix_k'], o['ix_w'], o['fa_f'], zeros(LANE - DI - HI - HA), zeros(NZ - Z_MISC - LANE)]
    wz = jnp.concatenate(parts, axis=0)
    assert wz.shape[0] == NZ
    return wz.astype(BF16)


def _rope_tables(pos):
    lane = jnp.arange(LANE)
    posf = pos.astype(F32)[:, None]

    def tab(head_w, rot, lane_limit):
        half = rot // 2
        lh = lane % head_w
        inv = ROPE_THETA ** (-(lh % half).astype(F32) * (2.0 / rot))
        ang = posf * inv[None, :]
        c, s = jnp.cos(ang), jnp.sin(ang)
        act = (lh < rot) & (lane < lane_limit)
        cos_t = jnp.where(act[None, :], c, 1.0)
        sin_lo = jnp.where((act & (lh >= half))[None, :], s, 0.0)
        sin_hi = jnp.where((act & (lh < half))[None, :], -s, 0.0)
        return [cos_t, sin_lo, sin_hi]

    return jnp.stack(tab(DH, ROT_DIM, LANE) + tab(DI, ROT_DIM_IDX, LANE) + tab(DI, ROT_DIM_IDX, DI), axis=0)


def _log_sigmoid(x):
    return jnp.minimum(x, 0.0) - jnp.log(1.0 + jnp.exp(-jnp.abs(x)))


def _rope_chunk(x, tab_ref, kind, half):
    c = tab_ref[3 * kind + 0]
    s_lo = tab_ref[3 * kind + 1]
    s_hi = tab_ref[3 * kind + 2]
    return x * c + pltpu.roll(x, half, 1) * s_lo + pltpu.roll(x, LANE - half, 1) * s_hi


def _inproj_kernel(x_ref, g_ref, w_ref, tab_ref, bf_ref, z_ref, xn_ref):
    j = pl.program_id(1)

    @pl.when(j == 0)
    def _():
        x = x_ref[...]
        ms = jnp.mean(x * x, axis=-1, keepdims=True)
        xn_ref[...] = (x * lax.rsqrt(ms + RMS_EPS) * g_ref[...]).astype(BF16)

    acc = lax.dot_general(xn_ref[...], w_ref[...], (((1,), (1,)), ((), ())), preferred_element_type=F32)
    n_chunk = TN // LANE
    for lo, hi, kind in TILE_KINDS:
        @pl.when((j >= lo) & (j < hi))
        def _(kind=kind):
            if kind == 0:
                z_ref[...] = acc
            elif kind in (1, 2):
                half = ROT_DIM // 2 if kind == 1 else ROT_DIM_IDX // 2
                for c in range(n_chunk):
                    sl = slice(c * LANE, (c + 1) * LANE)
                    z_ref[:, sl] = _rope_chunk(acc[:, sl], tab_ref, kind - 1, half)
            else:
                z_ref[...] = acc
                c = (Z_MISC - Z_RC_WA) // LANE
                sl = slice(c * LANE, (c + 1) * LANE)
                xm = acc[:, sl]
                roped = _rope_chunk(xm, tab_ref, 2, ROT_DIM_IDX // 2)
                lane = lax.broadcasted_iota(jnp.int32, xm.shape, 1)
                is_f = (lane >= M_F) & (lane < M_F + HA)
                z_ref[:, sl] = jnp.where(is_f, _log_sigmoid(xm + bf_ref[...]), roped)


def _inproj(x2d, gain, wz, tabs, bf_row, tm):
    rows, d = x2d.shape
    n_pos_blk = tabs.shape[1] // tm
    grid = (rows // tm, NZ // TN)
    return pl.pallas_call(
        _inproj_kernel,
        grid=grid,
        in_specs=[
            pl.BlockSpec((tm, d), lambda i, j: (i, 0)),
            pl.BlockSpec((1, d), lambda i, j: (0, 0)),
            pl.BlockSpec((TN, d), lambda i, j: (j, 0)),
            pl.BlockSpec((9, tm, LANE), lambda i, j: (0, i % n_pos_blk, 0)),
            pl.BlockSpec((1, LANE), lambda i, j: (0, 0)),
        ],
        out_specs=pl.BlockSpec((tm, TN), lambda i, j: (i, j)),
        out_shape=jax.ShapeDtypeStruct((rows, NZ), F32),
        scratch_shapes=[pltpu.VMEM((tm, d), BF16)],
        compiler_params=_cparams(("parallel", "arbitrary")),
    )(x2d, gain.reshape(1, d), wz, tabs, bf_row)


def _outproj_kernel(oa_ref, ob_ref, oc_ref, x_ref, w_ref, fg_ref, y_ref, *, final):
    acc = jnp.dot(oa_ref[...].astype(BF16), w_ref[0:WA, :], preferred_element_type=F32)
    acc += jnp.dot(ob_ref[...].astype(BF16), w_ref[WA:WA + WB, :], preferred_element_type=F32)
    acc += jnp.dot(oc_ref[...].astype(BF16), w_ref[WA + WB:, :], preferred_element_type=F32)
    y = x_ref[...] + acc
    if final:
        ms = jnp.mean(y * y, axis=-1, keepdims=True)
        y = y * lax.rsqrt(ms + RMS_EPS) * fg_ref[...]
    y_ref[...] = y


def _outproj(oa, ob, oc, x2d, w_bf, fgain, tm, final):
    rows, d = x2d.shape
    row_spec = lambda w: pl.BlockSpec((tm, w), lambda i: (i, 0))
    return pl.pallas_call(
        functools.partial(_outproj_kernel, final=final),
        grid=(rows // tm,),
        in_specs=[row_spec(WA), row_spec(WB), row_spec(WC), row_spec(d),
                  pl.BlockSpec((WA + WB + WC, d), lambda i: (0, 0)),
                  pl.BlockSpec((1, d), lambda i: (0, 0))],
        out_specs=row_spec(d),
        out_shape=jax.ShapeDtypeStruct((rows, d), F32),
        compiler_params=_cparams(("parallel",)),
    )(oa, ob, oc, x2d, w_bf, fgain.reshape(1, d))


def _cumsum_kernel(z_ref, cb_ref, carry_ref):
    @pl.when(pl.program_id(1) == 0)
    def _():
        carry_ref[...] = jnp.zeros_like(carry_ref)

    lf = z_ref[...]
    tm = lf.shape[0]
    r = lax.broadcasted_iota(jnp.int32, (tm, tm), 0)
    c = lax.broadcasted_iota(jnp.int32, (tm, tm), 1)
    tri = jnp.where(r >= c, 1.0, 0.0).astype(F32)
    cum = jnp.dot(tri, lf, precision=HIGHEST, preferred_element_type=F32) + carry_ref[...]
    carry_ref[...] = cum[tm - 1:tm, :]
    for h in range(HA):
        cb_ref[:, h * LANE:(h + 1) * LANE] = jnp.broadcast_to(cum[:, M_F + h:M_F + h + 1], (tm, LANE))


def _fox_cumsum(z, batch, seq, tm=256):
    nblk = seq // tm
    return pl.pallas_call(
        _cumsum_kernel,
        grid=(batch, nblk),
        in_specs=[pl.BlockSpec((tm, LANE), lambda b, i: (b * nblk + i, Z_MISC // LANE))],
        out_specs=pl.BlockSpec((tm, HA * LANE), lambda b, i: (b * nblk + i, 0)),
        out_shape=jax.ShapeDtypeStruct((batch * seq, HA * LANE), F32),
        scratch_shapes=[pltpu.VMEM((1, LANE), F32)],
        compiler_params=_cparams(("parallel", "arbitrary")),
    )(z)


def _silu(g):
    return g / (1.0 + jnp.exp(-g))


def _flash_kernel(*refs, mode, nh, tq):
    if mode == 'fox':
        q_ref, k_ref, v_ref, g_ref, cb_ref, o_ref, ko_ref, vo_ref, qs_ref, m_ref, l_ref, acc_ref = refs
    else:
        q_ref, k_ref, v_ref, g_ref, bias_ref, o_ref, ko_ref, vo_ref, qs_ref, m_ref, l_ref, acc_ref, bt_ref = refs
    qi = pl.program_id(1)
    ki = pl.program_id(2)
    nt = (((1,), (1,)), ((), ()))
    tn = (((0,), (0,)), ((), ()))

    @pl.when(ki == 0)
    def _():
        qs_ref[...] = (q_ref[...] * (DH ** -0.5)).astype(BF16)
        m_ref[...] = jnp.full_like(m_ref, -jnp.inf)
        l_ref[...] = jnp.zeros_like(l_ref)
        acc_ref[...] = jnp.zeros_like(acc_ref)

    def body(diag):
        if mode == 'dsa':
            bt_ref[...] = bias_ref[0].astype(F32)
        elif diag:
            key_i = lax.broadcasted_iota(jnp.int32, (tq, tq), 0)
            qry_i = lax.broadcasted_iota(jnp.int32, (tq, tq), 1)
            causal = key_i <= qry_i
        for h in range(nh):
            hs = slice(h * DH, (h + 1) * DH)
            st = lax.dot_general(k_ref[:, hs].astype(BF16), qs_ref[:, hs], nt, preferred_element_type=F32)
            if mode == 'fox':
                st = st - jnp.concatenate([cb_ref[:, hs]] * (tq // LANE), axis=1)
                if diag:
                    st = jnp.where(causal, st, NEG)
            else:
                st = st + bt_ref[...]
            m_prev = m_ref[h]
            m_new = jnp.maximum(m_prev, jnp.max(st, axis=0, keepdims=True))
            corr = jnp.exp(m_prev - m_new)
            p = jnp.exp(st - m_new)
            l_ref[h] = corr * l_ref[h] + jnp.sum(p, axis=0, keepdims=True)
            pv = lax.dot_general(v_ref[:, hs].astype(BF16), p.astype(BF16), tn, preferred_element_type=F32)
            acc_ref[hs, :] = corr * acc_ref[hs, :] + pv
            m_ref[h] = m_new

    @pl.when(ki < qi)
    def _():
        body(False)

    @pl.when(ki == qi)
    def _():
        body(True)
        for h in range(nh):
            hs = slice(h * DH, (h + 1) * DH)
            o_ref[:, hs] = (acc_ref[hs, :] / l_ref[h]).T * _silu(g_ref[:, hs])
            ko_ref[0, h] = k_ref[:, hs]
            vo_ref[0, h] = v_ref[:, hs]


def _flash(z, extra, batch, seq, mode, tq=512):
    nh = HA if mode == 'fox' else HB
    wgrp = nh * DH
    nq = seq // tq
    cq, ck, cv, cg = ((Z_FA_Q, Z_FA_K, Z_FA_V, Z_FA_G) if mode == 'fox' else (Z_SB_Q, Z_SB_K, Z_SB_V, Z_SB_G))
    qmap = lambda col: (lambda b, i, k: (b * nq + i, col // wgrp))
    kmap = lambda col: (lambda b, i, k: (b * nq + jnp.minimum(k, i), col // wgrp))
    in_specs = [pl.BlockSpec((tq, wgrp), qmap(cq)), pl.BlockSpec((tq, wgrp), kmap(ck)),
                pl.BlockSpec((tq, wgrp), kmap(cv)), pl.BlockSpec((tq, wgrp), qmap(cg))]
    scratch = [pltpu.VMEM((tq, wgrp), BF16), pltpu.VMEM((nh, 1, tq), F32), pltpu.VMEM((nh, 1, tq), F32),
               pltpu.VMEM((wgrp, tq), F32)]
    if mode == 'fox':
        in_specs += [pl.BlockSpec((tq, wgrp), lambda b, i, k: (b * nq + jnp.minimum(k, i), 0))]
    else:
        in_specs += [pl.BlockSpec((1, tq, tq), lambda b, i, k: (b, jnp.minimum(k, i), i))]
        scratch += [pltpu.VMEM((tq, tq), F32)]
    args = (z, z, z, z, extra)
    return pl.pallas_call(
        functools.partial(_flash_kernel, mode=mode, nh=nh, tq=tq),
        grid=(batch, nq, nq),
        in_specs=in_specs,
        out_specs=[pl.BlockSpec((tq, wgrp), lambda b, i, k: (b * nq + i, 0)),
                   pl.BlockSpec((1, nh, tq, DH), lambda b, i, k: (b, 0, i, 0)),
                   pl.BlockSpec((1, nh, tq, DH), lambda b, i, k: (b, 0, i, 0))],
        out_shape=[jax.ShapeDtypeStruct((batch * seq, wgrp), F32),
                   jax.ShapeDtypeStruct((batch, nh, seq, DH), F32),
                   jax.ShapeDtypeStruct((batch, nh, seq, DH), F32)],
        scratch_shapes=scratch,
        compiler_params=_cparams(("parallel", "parallel", "arbitrary")),
    )(*args)


def _sort_key(x):
    b = pltpu.bitcast(x + 0.0, jnp.int32)
    return b ^ ((b >> 31) & 0x7FFFFFFF)


def _kth_threshold(count_ge, shape, k):
    def step(it, u):
        bit = jnp.left_shift(jnp.int32(1), 31 - it)
        cand = u | bit
        cnt = count_ge(cand ^ INT_MIN)
        return jnp.where(cnt >= k, cand, u)

    u = lax.fori_loop(0, 32, step, jnp.zeros(shape, jnp.int32))
    return u ^ INT_MIN


def _dsa_select_kernel(qi_ref, qm_ref, km_ref, bias_ref, key_ref, qst_ref, *, tq, n_sel):
    i = pl.program_id(1)
    seq = km_ref.shape[0]
    n_kc = seq // tq
    nt = (((1,), (1,)), ((), ()))
    for h in range(HI):
        qst_ref[h] = qi_ref[...][:, h * DI:(h + 1) * DI].astype(BF16)
    wt = (qm_ref[...] * ((DI ** -0.5) * (HI ** -0.5))).T

    def score_chunk(c, _):
        c0 = pl.multiple_of(c * tq, tq)
        kc = km_ref[pl.ds(c0, tq), :][:, M_IXK:M_IXK + DI].astype(BF16)
        sc = jnp.zeros((tq, tq), F32)
        for h in range(HI):
            d = lax.dot_general(kc, qst_ref[h], nt, preferred_element_type=F32)
            sc = sc + jnp.maximum(d, 0.0) * wt[M_IXW + h:M_IXW + h + 1, :]
        key = _sort_key(sc)
        kr = lax.broadcasted_iota(jnp.int32, (tq, tq), 0)
        qc = lax.broadcasted_iota(jnp.int32, (tq, tq), 1)
        key_ref[pl.ds(c0, tq), :] = jnp.where((c < i) | (kr <= qc), key, INT_MIN)
        return 0

    lax.fori_loop(0, i + 1, score_chunk, 0)

    def count_ge(t):
        def chunk(c, acc):
            c0 = pl.multiple_of(c * tq, tq)
            hit = jnp.where(key_ref[pl.ds(c0, tq), :] >= t, 1.0, 0.0)
            return acc + jnp.sum(hit.reshape(tq // 8, 8, tq), axis=0)

        acc = lax.fori_loop(0, i + 1, chunk, jnp.zeros((8, tq), F32))
        return jnp.sum(acc, axis=0, keepdims=True)

    thr = _kth_threshold(count_ge, (1, tq), float(n_sel))

    def write_chunk(c, _):
        c0 = pl.multiple_of(c * tq, tq)
        kk = key_ref[pl.ds(c0, tq), :]
        sel = (kk >= thr) & (kk != INT_MIN)
        bias_ref[0, pl.ds(c0, tq), :] = jnp.where(sel, 0.0, NEG).astype(BF16)
        return 0

    def fill_chunk(c, _):
        c0 = pl.multiple_of(c * tq, tq)
        bias_ref[0, pl.ds(c0, tq), :] = jnp.full((tq, tq), NEG, BF16)
        return 0

    lax.fori_loop(0, i + 1, write_chunk, 0)
    lax.fori_loop(i + 1, n_kc, fill_chunk, 0)


def _dsa_select(z, batch, seq, tq=256):
    nq = seq // tq
    n_sel = min(TOPK, seq // 4)
    return pl.pallas_call(
        functools.partial(_dsa_select_kernel, tq=tq, n_sel=n_sel),
        grid=(batch, nq),
        in_specs=[pl.BlockSpec((tq, HI * DI), lambda b, i: (b * nq + i, Z_IX_Q // (HI * DI))),
                  pl.BlockSpec((tq, LANE), lambda b, i: (b * nq + i, Z_MISC // LANE)),
                  pl.BlockSpec((seq, LANE), lambda b, i: (b, Z_MISC // LANE))],
        out_specs=pl.BlockSpec((1, seq, tq), lambda b, i: (b, 0, i)),
        out_shape=jax.ShapeDtypeStruct((batch, seq, seq), BF16),
        scratch_shapes=[pltpu.VMEM((seq, tq), jnp.int32), pltpu.VMEM((HI, tq, DI), BF16)],
        compiler_params=_cparams(("parallel", "parallel")),
    )(z, z, z)


DIMS_NN = (((1,), (0,)), ((), ()))
DIMS_NT = (((1,), (1,)), ((), ()))
DIMS_TN = (((0,), (0,)), ((), ()))


def _dot_hi(a, b):
    return jnp.dot(a, b, precision=HIGHEST, preferred_element_type=F32)


def _split(x):
    hi = x.astype(BF16)
    lo = (x - hi.astype(F32)).astype(BF16)
    return hi, lo


def _dot3(a, b, dims=DIMS_NN):
    ah, al = a if isinstance(a, tuple) else _split(a)
    bh, bl = b if isinstance(b, tuple) else _split(b)
    d = lambda x, y: lax.dot_general(x, y, dims, preferred_element_type=F32)
    return d(ah, bh) + (d(ah, bl) + d(al, bh))


def _dot2(x, m_bf):
    hi, lo = _split(x)
    return (jnp.dot(hi, m_bf, preferred_element_type=F32) + jnp.dot(lo, m_bf, preferred_element_type=F32))


def _dot2l(m_bf, x):
    hi, lo = _split(x)
    return (jnp.dot(m_bf, hi, preferred_element_type=F32) + jnp.dot(m_bf, lo, preferred_element_type=F32))


def _rwkv_premix(cur, prev, p_ref, wb_ref, ab_ref, hones_ref):
    mix = lambda c, pv, mu: c + (pv - c) * mu
    xr = mix(cur['r'], prev['r'], p_ref[0:1, :])
    xk = mix(cur['k'], prev['k'], p_ref[1:2, :])
    xv = mix(cur['v'], prev['v'], p_ref[2:3, :])
    xwa = mix(cur['wa'], prev['wa'], p_ref[3:4, 0:LANE])
    hones = hones_ref[...]
    w_raw = p_ref[4:5, :] + _dot3(jnp.tanh(xwa), (wb_ref[0], wb_ref[1]))
    nw = -w_raw
    softplus = jnp.maximum(nw, 0.0) + jnp.log(1.0 + jnp.exp(-jnp.abs(nw)))
    lw = -jnp.exp(-softplus - 0.5)
    a = 1.0 / (1.0 + jnp.exp(-(p_ref[5:6, :] + _dot3(xwa, (ab_ref[0], ab_ref[1])))))
    kk0 = xk * p_ref[6:7, :]
    ss = _dot2(kk0 * kk0, hones)
    kk = kk0 / jnp.maximum(jnp.sqrt(ss), 1e-12)
    k2 = xk * (1.0 + (a - 1.0) * p_ref[7:8, :])
    bonus = _dot2(xr * k2 * p_ref[8:9, :], hones) * xv
    return xr, k2, xv, lw, kk, a, bonus


def _rwkv_finish(out, bonus, g, p_ref, hones_ref):
    hones = hones_ref[...]
    mean = _dot2(out, hones) * (1.0 / DC)
    d = out - mean
    var = _dot2(d * d, hones) * (1.0 / DC)
    y = d * lax.rsqrt(var + GN_EPS) * p_ref[9:10, :] + p_ref[10:11, :] + bonus
    return y * _silu(g)


def _rwkv_params(mu, w0, w_b, a0, a_b, k_k, k_a, r_k, ln_w, ln_b):
    pad = lambda v: jnp.concatenate([v, jnp.zeros((WC - v.shape[0],), F32)])
    rows = [mu[:WC], mu[WC:2 * WC], mu[2 * WC:3 * WC], pad(mu[3 * WC:]), w0, a0, k_k, k_a,
            r_k.reshape(WC), ln_w, ln_b]
    rows += [jnp.zeros((WC,), F32)] * (16 - len(rows))
    p = jnp.stack(rows, axis=0)
    split2 = lambda m: jnp.stack(_split(m), axis=0)
    wb = split2(jnp.zeros((LANE, WC), F32).at[0:LORA].set(w_b))
    ab = split2(jnp.zeros((LANE, WC), F32).at[LORA:2 * LORA].set(a_b))
    hid = jnp.arange(WC) // DC
    hones = (hid[:, None] == hid[None, :]).astype(BF16)
    return p, wb, ab, hones


def _rwkv_chunk_terms(units, first, strict, incl, eye):
    C = units[0][0].shape[0]
    cat0 = lambda xs: jnp.concatenate(xs, axis=0)
    cat1 = lambda xs: jnp.concatenate(xs, axis=1)
    stack = lambda x: cat0([jnp.where(first, x, 0.0), jnp.where(first, 0.0, x)])
    dup = lambda x: cat0([x, x])
    la = [stack(u[0]) for u in units]
    lr = [stack(u[1]) for u in units]
    vs = [stack(u[6]) for u in units]
    xx = [_dot3(cat0([a, r]), cat0([dup(u[2]), dup(u[3])]), DIMS_NT) for a, r, u in zip(la, lr, units)]
    n_m = [jnp.where(strict, x[:2 * C, :2 * C], 0.0) for x in xx]
    m_m = [jnp.where(strict, x[:2 * C, 2 * C:], 0.0) for x in xx]
    pb = [jnp.where(incl, x[2 * C:, :2 * C], 0.0) for x in xx]
    pk = [jnp.where(incl, x[2 * C:, 2 * C:], 0.0) for x in xx]
    t_inv = [jnp.where(eye, 1.0, 0.0) + n for n in n_m]
    pw_s = [_split(n) for n in n_m]
    sz = 2
    while sz < C:
        pw_s = [_split(_dot3(s, s)) for s in pw_s]
        t_inv = [t + _dot3(s, t) for s, t in zip(pw_s, t_inv)]
        sz *= 2
    mv = [_dot3(m, v) for m, v in zip(m_m, vs)]
    ta = [_dot3(t, cat1([a, m])) for t, a, m in zip(t_inv, la, mv)]
    big = [_split(cat0([t, cat1([jnp.zeros_like(v), v])])) for t, v in zip(ta, vs)]
    pp = [_dot3(cat1([b, k]), g) for b, k, g in zip(pb, pk, big)]
    php = [_dot3(cat0([stack(u[4]), stack(u[5])]), g, DIMS_TN) for u, g in zip(units, big)]
    terms = []
    for u, r, p, h in zip(units, lr, pp, php):
        r1s = r + p[:, :LANE]
        ovs = p[:, LANE:]
        phi = jnp.where(eye, jnp.broadcast_to(u[7], (LANE, LANE)), 0.0) + h[:, :LANE]
        terms.append((cat0([phi, r1s[:C] + r1s[C:]]), ovs[:C] + ovs[C:], h[:, LANE:]))
    return terms


def _rwkv_chunk_kernel(r_ref, k_ref, v_ref, wa_ref, g_ref, p_ref, wb_ref, ab_ref, hones_ref,
                       o_ref, st_ref, s_ref, pr_ref, pk_ref, pv_ref, pwa_ref):
    step = pl.program_id(1)
    R = r_ref.shape[0]
    C = RC_CHUNK
    ncs = R // C

    @pl.when(step == 0)
    def _():
        s_ref[...] = jnp.zeros_like(s_ref)
        pr_ref[...] = jnp.zeros_like(pr_ref)
        pk_ref[...] = jnp.zeros_like(pk_ref)
        pv_ref[...] = jnp.zeros_like(pv_ref)
        pwa_ref[...] = jnp.zeros_like(pwa_ref)

    def shifted(x, carry_ref):
        row = lax.broadcasted_iota(jnp.int32, x.shape, 0)
        pv = jnp.where(row == 0, carry_ref[...], pltpu.roll(x, 1, 0))
        carry_ref[...] = x[R - 1:R, :]
        return pv

    cur = {'r': r_ref[...], 'k': k_ref[...], 'v': v_ref[...], 'wa': wa_ref[...]}
    prev = {'r': shifted(cur['r'], pr_ref), 'k': shifted(cur['k'], pk_ref),
            'v': shifted(cur['v'], pv_ref), 'wa': shifted(cur['wa'], pwa_ref)}
    xr, k2, xv, lw, kk, a, bonus = _rwkv_premix(cur, prev, p_ref, wb_ref, ab_ref, hones_ref)

    ri = lax.broadcasted_iota(jnp.int32, (R, R), 0)
    ci = lax.broadcasted_iota(jnp.int32, (R, R), 1)
    tril = jnp.where((ri >= ci) & ((ri // C) == (ci // C)), 1.0, 0.0).astype(BF16)
    lc = _dot2l(tril, lw)
    lend = jnp.concatenate([jnp.broadcast_to(lc[(c + 1) * C - 1:(c + 1) * C, :], (C, WC)) for c in range(ncs)],
                           axis=0)
    e_neg = jnp.exp(-lc)
    e_end = jnp.exp(lend - lc)
    bn = kk * a
    a_t = -kk * jnp.exp(lc - lw)
    r_t = xr * jnp.exp(lc)
    b_t = bn * e_neg
    k_t = k2 * e_neg
    b_h = bn * e_end
    k_h = k2 * e_end

    first = lax.broadcasted_iota(jnp.int32, (1, LANE), 1) < DC
    pr_i = lax.broadcasted_iota(jnp.int32, (LANE, LANE), 0)
    pc_i = lax.broadcasted_iota(jnp.int32, (LANE, LANE), 1)
    same = (pr_i // DC) == (pc_i // DC)
    strict = same & (pr_i > pc_i)
    incl = same & (pr_i >= pc_i)
    eye = pr_i == pc_i

    units = []
    for c in range(ncs):
        rs = slice(c * C, (c + 1) * C)
        g_row = jnp.exp(lc[(c + 1) * C - 1:(c + 1) * C, :])
        for p in range(HC // 2):
            ps = slice(p * LANE, (p + 1) * LANE)
            units.append((a_t[rs, ps], r_t[rs, ps], b_t[rs, ps], k_t[rs, ps], b_h[rs, ps], k_h[rs, ps],
                          xv[rs, ps], g_row[:, ps]))
    terms = _rwkv_chunk_terms(units, first, strict, incl, eye)

    states = [s_ref[p] for p in range(HC // 2)]
    out_rows = []
    for c in range(ncs):
        outs = []
        for p in range(HC // 2):
            lhs, ov, psi = terms[c * (HC // 2) + p]
            sr = _dot3(lhs, states[p])
            outs.append(sr[LANE:] + ov)
            states[p] = sr[:LANE] + psi
        out_rows.append(jnp.concatenate(outs, axis=1))
    out = jnp.concatenate(out_rows, axis=0)
    o_ref[...] = _rwkv_finish(out, bonus, g_ref[...], p_ref, hones_ref)
    for p in range(HC // 2):
        s_ref[p] = states[p]

    @pl.when(step == pl.num_programs(1) - 1)
    def _():
        for p in range(HC // 2):
            st_ref[0, p] = states[p]


def _rwkv_prompt(z, params, batch, seq):
    p, wb, ab, hones = params
    R = min(RC_ROWS, seq)
    nc = seq // R
    col = lambda off, w: (lambda b, i: (b * nc + i, off // w))
    const = lambda shp: pl.BlockSpec(shp, lambda b, i: (0,) * len(shp))
    o, st = pl.pallas_call(
        _rwkv_chunk_kernel,
        grid=(batch, nc),
        in_specs=[pl.BlockSpec((R, WC), col(Z_RC_R, WC)), pl.BlockSpec((R, WC), col(Z_RC_K, WC)),
                  pl.BlockSpec((R, WC), col(Z_RC_V, WC)), pl.BlockSpec((R, LANE), col(Z_RC_WA, LANE)),
                  pl.BlockSpec((R, WC), col(Z_RC_G, WC)),
                  const((16, WC)), const((2, LANE, WC)), const((2, LANE, WC)), const((WC, WC))],
        out_specs=[pl.BlockSpec((R, WC), lambda b, i: (b * nc + i, 0)),
                   pl.BlockSpec((1, HC // 2, LANE, LANE), lambda b, i: (b, 0, 0, 0))],
        out_shape=[jax.ShapeDtypeStruct((batch * seq, WC), F32),
                   jax.ShapeDtypeStruct((batch, HC // 2, LANE, LANE), F32)],
        scratch_shapes=[pltpu.VMEM((HC // 2, LANE, LANE), F32), pltpu.VMEM((1, WC), F32),
                        pltpu.VMEM((1, WC), F32), pltpu.VMEM((1, WC), F32), pltpu.VMEM((1, LANE), F32)],
        compiler_params=_cparams(("parallel", "arbitrary")),
    )(z, z, z, z, z, p, wb, ab, hones)
    st = st.reshape(batch, HC // 2, 2, DC, 2, DC)
    st = jnp.stack([st[:, :, 0, :, 0, :], st[:, :, 1, :, 1, :]], axis=2).reshape(batch, HC, DC, DC)
    return o, jnp.swapaxes(st, -1, -2)


def _paged_attn_kernel(*refs, mode, pps, nh):
    pt_ref = refs[0]
    q_ref, kn_ref, vn_ref, g_ref = refs[1:5]
    k_refs = refs[5:5 + pps]
    v_refs = refs[5 + pps:5 + 2 * pps]
    rest = refs[5 + 2 * pps:]
    if mode == 'fox':
        lf_refs = rest[:pps]
        lfn_ref = rest[pps]
        o_ref, qm_ref, m_ref, l_ref, acc_ref, carry_ref = rest[pps + 1:]
    else:
        bias_ref, biasn_ref = rest[:2]
        o_ref, qm_ref, m_ref, l_ref, acc_ref = rest[2:]
    del pt_ref
    step = pl.program_id(1)
    w = nh * DH
    hrow = lax.broadcasted_iota(jnp.int32, (8, w), 0)
    hcol = lax.broadcasted_iota(jnp.int32, (8, w), 1) // DH
    own = hrow == hcol

    @pl.when(step == 0)
    def _():
        qm_ref[...] = jnp.where(own, q_ref[0] * (DH ** -0.5), 0.0).astype(BF16)
        m_ref[...] = jnp.full_like(m_ref, -jnp.inf)
        l_ref[...] = jnp.zeros_like(l_ref)
        acc_ref[...] = jnp.zeros_like(acc_ref)
        if mode == 'fox':
            carry_ref[...] = jnp.zeros_like(carry_ref)

    nt = (((1,), (1,)), ((), ()))
    qm = qm_ref[...]
    s_parts = []
    if mode == 'fox':
        ur = lax.broadcasted_iota(jnp.int32, (LANE, LANE), 0)
        uc = lax.broadcasted_iota(jnp.int32, (LANE, LANE), 1)
        upper = jnp.where(ur <= uc, 1.0, 0.0).astype(F32)
        carry = carry_ref[...]
    cat = lambda ref: jnp.concatenate([ref[h] for h in range(nh)], axis=1).astype(BF16)
    for i in range(pps):
        s = lax.dot_general(qm, cat(k_refs[i]), nt, preferred_element_type=F32)
        if mode == 'fox':
            cum = _dot_hi(lf_refs[i][...], upper) + carry
            carry = jnp.broadcast_to(cum[:, LANE - 1:LANE], (8, LANE))
            s = s - cum
        else:
            s = s + bias_ref[0, :, i * LANE:(i + 1) * LANE]
        s_parts.append(s)
    if mode == 'fox':
        carry_ref[...] = carry
    s_all = jnp.concatenate(s_parts, axis=1)
    m_prev = m_ref[...]
    m_new = jnp.maximum(m_prev, jnp.max(s_all, axis=1, keepdims=True))
    corr = jnp.exp(m_prev - m_new)
    p_all = jnp.exp(s_all - m_new)
    l_ref[...] = corr * l_ref[...] + jnp.sum(p_all, axis=1, keepdims=True)
    pv = jnp.zeros((8, w), F32)
    for i in range(pps):
        pv = pv + jnp.dot(p_all[:, i * LANE:(i + 1) * LANE].astype(BF16), cat(v_refs[i]),
                          preferred_element_type=F32)
    acc_ref[...] = corr * acc_ref[...] + pv
    m_ref[...] = m_new

    @pl.when(step == pl.num_programs(1) - 1)
    def _():
        kn = kn_ref[0].astype(BF16).astype(F32)
        sn = jnp.sum(qm_ref[...].astype(F32) * kn, axis=1, keepdims=True)
        if mode == 'fox':
            sn = sn - (carry_ref[:, 0:1] + lfn_ref[0])
        else:
            sn = sn + biasn_ref[0, :, 0:1]
        m_p = m_ref[...]
        m_n = jnp.maximum(m_p, sn)
        cr = jnp.exp(m_p - m_n)
        pn = jnp.exp(sn - m_n)
        lsum = cr * l_ref[...] + pn
        acc = cr * acc_ref[...] + pn * vn_ref[0]
        o = jnp.sum(jnp.where(own, acc / lsum, 0.0), axis=0, keepdims=True)
        o_ref[0] = o * _silu(g_ref[0])


def _paged_attn(zs3, cache_k, cache_v, layer, page_table, extra, mode):
    db = zs3.shape[0]
    n_pages = page_table.shape[1]
    pps = min(PPS_ATTN, n_pages)
    nsteps = n_pages // pps
    nh = HA if mode == 'fox' else HB
    w = nh * DH
    cq, ck, cv, cg = ((Z_FA_Q, Z_FA_K, Z_FA_V, Z_FA_G) if mode == 'fox' else (Z_SB_Q, Z_SB_K, Z_SB_V, Z_SB_G))
    zspec = lambda col: pl.BlockSpec((1, 1, w), lambda b, s, pt: (b, 0, col // w))
    page_map = lambda i: (lambda b, s, pt: (layer, pt[b * n_pages + s * pps + i], 0, 0))
    page_map5 = lambda i: (lambda b, s, pt: (layer, pt[b * n_pages + s * pps + i], 0, 0, 0))
    in_specs = [zspec(cq), zspec(ck), zspec(cv), zspec(cg)]
    in_specs += [pl.BlockSpec((None, None, nh, PAGE, DH), page_map5(i)) for i in range(pps)]
    in_specs += [pl.BlockSpec((None, None, nh, PAGE, DH), page_map5(i)) for i in range(pps)]
    args = [zs3, zs3, zs3, zs3] + [cache_k] * pps + [cache_v] * pps
    scratch = [pltpu.VMEM((8, w), BF16), pltpu.VMEM((8, 1), F32), pltpu.VMEM((8, 1), F32),
               pltpu.VMEM((8, w), F32)]
    if mode == 'fox':
        logf_t, lf_new = extra
        in_specs += [pl.BlockSpec((None, None, 8, PAGE), page_map(i)) for i in range(pps)]
        in_specs += [pl.BlockSpec((1, 8, 1), lambda b, s, pt: (b, 0, 0))]
        args += [logf_t] * pps + [lf_new]
        scratch += [pltpu.VMEM((8, LANE), F32)]
    else:
        bias = extra
        in_specs += [pl.BlockSpec((1, 1, pps * LANE), lambda b, s, pt: (b, 0, s)),
                     pl.BlockSpec((1, 1, LANE), lambda b, s, pt: (b, 0, n_pages))]
        args += [bias, bias]
    grid_spec = pltpu.PrefetchScalarGridSpec(
        num_scalar_prefetch=1, grid=(db, nsteps), in_specs=in_specs,
        out_specs=pl.BlockSpec((1, 1, w), lambda b, s, pt: (b, 0, 0)),
        scratch_shapes=scratch)
    return pl.pallas_call(
        functools.partial(_paged_attn_kernel, mode=mode, pps=pps, nh=nh),
        grid_spec=grid_spec,
        out_shape=jax.ShapeDtypeStruct((db, 1, w), F32),
        compiler_params=_cparams(("parallel", "arbitrary")),
    )(page_table.reshape(-1), *args)


def _sample_select_kernel(*refs, pps, n_sel):
    pt_ref = refs[0]
    qi_ref, wi_ref, kin_ref = refs[1:4]
    ki_refs = refs[4:4 + pps]
    bias_ref, key_ref = refs[4 + pps:]
    del pt_ref
    step = pl.program_id(1)
    nsteps = pl.num_programs(1)
    past = key_ref.shape[1] - LANE
    qb = qi_ref[0].astype(BF16)
    w = wi_ref[0] * ((DI ** -0.5) * (HI ** -0.5))

    def score(kmat_t):
        d = jnp.dot(qb, kmat_t.astype(BF16), preferred_element_type=F32)
        return jnp.sum(jnp.maximum(d, 0.0) * w, axis=0, keepdims=True)

    for i in range(pps):
        off = pl.multiple_of((step * pps + i) * LANE, LANE)
        key_ref[:, pl.ds(off, LANE)] = _sort_key(score(ki_refs[i][...]))

    @pl.when(step == nsteps - 1)
    def _():
        dn = jnp.sum(qb.astype(F32) * kin_ref[0].astype(BF16).astype(F32), axis=1, keepdims=True)
        sn = jnp.sum(jnp.maximum(dn, 0.0) * w, axis=0, keepdims=True)
        lane = lax.broadcasted_iota(jnp.int32, (1, LANE), 1)
        key_ref[:, past:past + LANE] = jnp.where(lane == 0, _sort_key(jnp.broadcast_to(sn, (1, LANE))), INT_MIN)
        bias_ref[0] = key_ref[...]


def _sample_threshold_kernel(key_ref, bias_ref, *, n_sel):
    keys = key_ref[...]

    def count_ge(t):
        return jnp.sum(jnp.where(keys >= t, 1.0, 0.0), axis=1, keepdims=True)

    thr = _kth_threshold(count_ge, (keys.shape[0], 1), float(n_sel))
    sel = (keys >= thr) & (keys != INT_MIN)
    bias_ref[...] = jnp.where(sel, 0.0, NEG)


def _sample_select(qi3, wi3, kin3, cache_ki, layer, page_table):
    db = qi3.shape[0]
    n_pages = page_table.shape[1]
    pps = min(PPS_IDX, n_pages)
    past = n_pages * PAGE
    n_sel = min(TOPK, (past + 1) // 4)
    page_map = lambda i: (lambda b, s, pt: (layer, pt[b * n_pages + s * pps + i], 0, 0))
    in_specs = [pl.BlockSpec((1, HI, DI), lambda b, s, pt: (b, 0, 0)),
                pl.BlockSpec((1, HI, 1), lambda b, s, pt: (b, 0, 0)),
                pl.BlockSpec((1, 1, DI), lambda b, s, pt: (b, 0, 0))]
    in_specs += [pl.BlockSpec((None, None, DI, PAGE), page_map(i)) for i in range(pps)]
    grid_spec = pltpu.PrefetchScalarGridSpec(
        num_scalar_prefetch=1, grid=(db, n_pages // pps), in_specs=in_specs,
        out_specs=pl.BlockSpec((1, 1, past + LANE), lambda b, s, pt: (b, 0, 0)),
        scratch_shapes=[pltpu.VMEM((1, past + LANE), jnp.int32)])
    keys = pl.pallas_call(
        functools.partial(_sample_select_kernel, pps=pps, n_sel=n_sel),
        grid_spec=grid_spec,
        out_shape=jax.ShapeDtypeStruct((db, 1, past + LANE), jnp.int32),
        compiler_params=_cparams(("parallel", "arbitrary")),
    )(page_table.reshape(-1), qi3, wi3, kin3, *([cache_ki] * pps))
    bias = pl.pallas_call(
        functools.partial(_sample_threshold_kernel, n_sel=n_sel),
        grid=(1,),
        in_specs=[pl.BlockSpec((db, past + LANE), lambda i: (0, 0))],
        out_specs=pl.BlockSpec((db, past + LANE), lambda i: (0, 0)),
        out_shape=jax.ShapeDtypeStruct((db, past + LANE), F32),
        compiler_params=_cparams(("arbitrary",)),
    )(keys.reshape(db, past + LANE))
    return bias.reshape(db, 1, past + LANE)


def _rwkv_pre_kernel(r_ref, k_ref, v_ref, wa_ref, sr_ref, sk_ref, sv_ref, swa_ref, p_ref, wb_ref, ab_ref,
                     hones_ref, or_ref, ok_ref, ov_ref, ow_ref, onk_ref, oka_ref, ob_ref):
    cur = {'r': r_ref[...], 'k': k_ref[...], 'v': v_ref[...], 'wa': wa_ref[...]}
    prev = {'r': sr_ref[...], 'k': sk_ref[...], 'v': sv_ref[...], 'wa': swa_ref[...]}
    xr, k2, xv, lw, kk, a, bonus = _rwkv_premix(cur, prev, p_ref, wb_ref, ab_ref, hones_ref)
    or_ref[...] = xr
    ok_ref[...] = k2
    ov_ref[...] = xv
    ow_ref[...] = jnp.exp(lw)
    onk_ref[...] = -kk
    oka_ref[...] = kk * a
    ob_ref[...] = bonus


def _rwkv_step_kernel(s_ref, r_ref, k_ref, v_ref, w_ref, nk_ref, ka_ref, so_ref, o_ref):
    s = s_ref[...]
    sa = jnp.sum(s * nk_ref[...], axis=-1, keepdims=True)
    s2 = s * w_ref[...] + sa * ka_ref[...] + v_ref[...] * k_ref[...]
    so_ref[...] = s2
    o_ref[...] = jnp.sum(s2 * r_ref[...], axis=-1, keepdims=True)


def _rwkv_post_kernel(o_ref, b_ref, g_ref, p_ref, hones_ref, y_ref):
    y_ref[...] = _rwkv_finish(o_ref[...], b_ref[...], g_ref[...], p_ref, hones_ref)


def _rwkv_sample(zs, shift_prev, wkv, params, bb=8):
    p, wb, ab, hones = params
    db = zs.shape[0]
    full = lambda a: pl.BlockSpec(a.shape, lambda i: (0,) * a.ndim)
    zcol = lambda off, w: pl.BlockSpec((db, w), lambda i: (0, off // w))
    swa = jnp.concatenate([shift_prev[:, 3 * WC:], jnp.zeros((db, LANE - 2 * LORA), F32)], axis=1)
    sr, sk, sv = shift_prev[:, :WC], shift_prev[:, WC:2 * WC], shift_prev[:, 2 * WC:3 * WC]
    rows = pl.pallas_call(
        _rwkv_pre_kernel,
        grid=(1,),
        in_specs=[zcol(Z_RC_R, WC), zcol(Z_RC_K, WC), zcol(Z_RC_V, WC), zcol(Z_RC_WA, LANE),
                  full(sr), full(sk), full(sv), full(swa), full(p), full(wb), full(ab), full(hones)],
        out_specs=[pl.BlockSpec((db, WC), lambda i: (0, 0))] * 7,
        out_shape=[jax.ShapeDtypeStruct((db, WC), F32)] * 7,
        compiler_params=_cparams(("arbitrary",)),
    )(zs, zs, zs, zs, sr, sk, sv, swa, p, wb, ab, hones)
    xr, k2, xv, dec, nkk, kka, bonus = rows
    as_row = lambda t: t.reshape(db, HC, 1, DC)
    rspec = pl.BlockSpec((bb, HC, 1, DC), lambda i: (i, 0, 0, 0))
    cspec = pl.BlockSpec((bb, HC, DC, 1), lambda i: (i, 0, 0, 0))
    sspec = pl.BlockSpec((bb, HC, DC, DC), lambda i: (i, 0, 0, 0))
    s_new, o_col = pl.pallas_call(
        _rwkv_step_kernel,
        grid=(db // bb,),
        in_specs=[sspec, rspec, rspec, cspec, rspec, rspec, rspec],
        out_specs=[sspec, cspec],
        out_shape=[jax.ShapeDtypeStruct((db, HC, DC, DC), F32), jax.ShapeDtypeStruct((db, HC, DC, 1), F32)],
        compiler_params=_cparams(("parallel",)),
    )(wkv, as_row(xr), as_row(k2), xv.reshape(db, HC, DC, 1), as_row(dec), as_row(nkk), as_row(kka))
    o_rows = o_col.reshape(db, WC)
    y = pl.pallas_call(
        _rwkv_post_kernel,
        grid=(1,),
        in_specs=[full(o_rows), full(bonus), zcol(Z_RC_G, WC), full(p), full(hones)],
        out_specs=pl.BlockSpec((db, WC), lambda i: (0, 0)),
        out_shape=jax.ShapeDtypeStruct((db, WC), F32),
        compiler_params=_cparams(("arbitrary",)),
    )(o_rows, bonus, zs, p, hones)
    return y, s_new


def _shift_cols(z_last):
    return jnp.concatenate([z_last[..., Z_RC_R:Z_RC_R + 3 * WC], z_last[..., Z_RC_WA:Z_RC_WA + 2 * LORA]], axis=-1)


def _layer_prompt(x2d, batch, seq, gain, wz, bf_row, tabs, rw_params, wo_bf, fgain, final):
    z = _inproj(x2d, gain, wz, tabs, bf_row, tm=min(INPROJ_TM, seq))
    c_t = _fox_cumsum(z, batch, seq, tm=min(CUM_TM, seq))
    o_a, fk, fv = _flash(z, c_t, batch, seq, 'fox', tq=min(FLASH_TQ, seq))
    bias = _dsa_select(z, batch, seq, tq=min(SEL_TQ, seq))
    o_b, dk, dv = _flash(z, bias, batch, seq, 'dsa', tq=min(FLASH_TQ, seq))
    o_c, wkv = _rwkv_prompt(z, rw_params, batch, seq)
    y = _outproj(o_a, o_b, o_c, x2d, wo_bf, fgain, tm=min(OUT_TM, seq), final=final)
    z3 = z.reshape(batch, seq, NZ)
    st = (fk, fv, z3[..., Z_MISC + M_F:Z_MISC + M_F + HA], dk, dv,
          z3[..., Z_MISC + M_IXK:Z_MISC + M_IXK + DI], wkv, _shift_cols(z3[:, -1]))
    return y, st


def _layer_sample(xs2d, layer, caches, page_table, gain, wz, bf_row, tabs, rw_params, wo_bf, fgain, final):
    c_fk, c_fv, c_flt, c_dk, c_dv, c_di, s_wkv, s_shift = caches
    db = xs2d.shape[0]
    zs = _inproj(xs2d, gain, wz, tabs, bf_row, tm=db)
    zs3 = zs.reshape(db, 1, NZ)
    misc = zs[:, Z_MISC:Z_MISC + LANE]
    lf_new = jnp.concatenate([misc[:, M_F:M_F + HA], jnp.zeros((db, 8 - HA), F32)], axis=1).reshape(db, 8, 1)
    o_a = _paged_attn(zs3, c_fk, c_fv, layer, page_table, (c_flt, lf_new), 'fox')
    qi3 = zs[:, Z_IX_Q:Z_IX_Q + HI * DI].reshape(db, HI, DI)
    wi3 = misc[:, M_IXW:M_IXW + HI].reshape(db, HI, 1)
    kin3 = misc[:, M_IXK:M_IXK + DI].reshape(db, 1, DI)
    bias = _sample_select(qi3, wi3, kin3, c_di, layer, page_table)
    o_b = _paged_attn(zs3, c_dk, c_dv, layer, page_table, bias, 'dsa')
    o_c, wkv = _rwkv_sample(zs, s_shift[layer], s_wkv[layer], rw_params)
    y = _outproj(o_a.reshape(db, WA), o_b.reshape(db, WB), o_c, xs2d, wo_bf, fgain, tm=db, final=final)
    st = (zs[:, Z_FA_K:Z_FA_K + WA].reshape(db, 1, HA, DH),
          zs[:, Z_FA_V:Z_FA_V + WA].reshape(db, 1, HA, DH),
          misc[:, M_F:M_F + HA].reshape(db, 1, HA),
          zs[:, Z_SB_K:Z_SB_K + WB].reshape(db, 1, HB, DH),
          zs[:, Z_SB_V:Z_SB_V + WB].reshape(db, 1, HB, DH),
          misc[:, M_IXK:M_IXK + DI].reshape(db, 1, DI),
          wkv, _shift_cols(zs))
    return y, st


def kernel(x_prompt, x_sample, cache_fox_k, cache_fox_v, cache_fox_logf, cache_dsa_k, cache_dsa_v, cache_dsa_kidx, state_rwkv_wkv, state_rwkv_shift, page_table, norm_gain, w_in, fox_forget_bias, rwkv_mu, rwkv_w0, rwkv_w_lora_b, rwkv_a0, rwkv_a_lora_b, rwkv_k_k, rwkv_k_a, rwkv_r_k, rwkv_ln_w, rwkv_ln_b, w_out, final_gain):
    batch, seq, d = x_prompt.shape
    db = x_sample.shape[0]
    assert x_sample.shape[1] == 1
    depth = w_in.shape[0]
    past = page_table.shape[1] * PAGE

    tabs_p = _rope_tables(jnp.arange(seq))
    tabs_s = _rope_tables(jnp.full((db,), past))
    head_major = lambda c: jnp.transpose(c, (0, 1, 3, 2, 4))
    c_flt = jnp.pad(jnp.swapaxes(cache_fox_logf, -1, -2), ((0, 0), (0, 0), (0, 8 - HA), (0, 0)))
    caches = (head_major(cache_fox_k), head_major(cache_fox_v), c_flt, head_major(cache_dsa_k),
              head_major(cache_dsa_v), jnp.swapaxes(cache_dsa_kidx, -1, -2), state_rwkv_wkv, state_rwkv_shift)

    xp = x_prompt.reshape(batch * seq, d)
    xs = x_sample.reshape(db, d)
    new_p, new_s = [], []
    for l in range(depth):
        final = l == depth - 1
        wz = _relayout_w_in(w_in[l])
        wo_bf = w_out[l].astype(BF16)
        bf_row = jnp.zeros((1, LANE), F32).at[0, M_F:M_F + HA].set(fox_forget_bias[l])
        rw_params = _rwkv_params(rwkv_mu[l], rwkv_w0[l], rwkv_w_lora_b[l], rwkv_a0[l], rwkv_a_lora_b[l],
                                 rwkv_k_k[l], rwkv_k_a[l], rwkv_r_k[l], rwkv_ln_w[l], rwkv_ln_b[l])
        xp, st_p = _layer_prompt(xp, batch, seq, norm_gain[l], wz, bf_row, tabs_p, rw_params, wo_bf,
                                 final_gain, final)
        xs, st_s = _layer_sample(xs, l, caches, page_table, norm_gain[l], wz, bf_row, tabs_s, rw_params,
                                 wo_bf, final_gain, final)
        new_p.append(st_p)
        new_s.append(st_s)
    outs_p = [jnp.stack(a) for a in zip(*new_p)]
    for i in (0, 1, 3, 4):
        outs_p[i] = jnp.transpose(outs_p[i], (0, 1, 3, 2, 4))
    outs_s = [jnp.stack(a) for a in zip(*new_s)]
    return (xp.reshape(batch, seq, d), xs.reshape(db, 1, d), *outs_p, *outs_s)
```
